```python
import jax, jax.numpy as jnp
from jax import lax
import numpy as np

D_MODEL = 1024
BATCH = 1
SEQ = 16384
DEPTH = 2
DEC_BATCH = 128
DEC_SEQ = 8
PAST_LEN = 16384
PAGE_SIZE = 128

H_RET = 6
DK_RET = 64
DV_RET = 64
RET_CHUNK = 128
H_DSA = 4
HD_DSA = 64
H_IDX = 8
D_IDX = 64
TOPK_MAX = 256
H_MLA = 6
D_Q_LAT = 256
D_KV_LAT = 128
D_NOPE = 64
D_ROPE = 32
DV_MLA = 64
D_MIX = H_RET * DV_RET + H_DSA * HD_DSA + H_MLA * DV_MLA
N_GROUPS = 4
EXPERTS_PER_GROUP = 8
N_EXPERTS = N_GROUPS * EXPERTS_PER_GROUP
TOP_K_IN_GROUP = 2
D_EXPERT = 256
Q_BLOCK = 128
ROPE_THETA = 10000.0
EPS = 1e-6
DSA_ROW = 2 * HD_DSA + D_IDX
MLA_ROW = D_KV_LAT + D_ROPE
IN_SPLITS = (H_RET * DK_RET, H_RET * DK_RET, H_RET * DV_RET, H_RET * DV_RET,
             H_DSA * HD_DSA, HD_DSA, HD_DSA, H_IDX * D_IDX, D_IDX, H_IDX,
             D_Q_LAT, D_KV_LAT, D_ROPE)
D_IN = sum(IN_SPLITS)

kernel_name = 'hymba_style_retention_dsa_mla_hmoe_step'


def rmsnorm(x, g):
    xf = x.astype(jnp.float32)
    y = xf * lax.rsqrt(jnp.mean(xf * xf, axis=-1, keepdims=True) + EPS)
    return (y * g.astype(jnp.float32)).astype(x.dtype)


def modulate(x, g, shift, scale):
    return rmsnorm(x, g) * (1.0 + scale[:, None, :]) + shift[:, None, :]


def rope(x, pos):
    d = x.shape[-1]
    half = d // 2
    inv = ROPE_THETA ** (-jnp.arange(half, dtype=jnp.float32) * 2.0 / d)
    ang = pos.astype(jnp.float32)[:, None] * inv[None, :]
    shape = (1, pos.shape[0]) + (1,) * (x.ndim - 3) + (half,)
    cos = jnp.cos(ang).reshape(shape)
    sin = jnp.sin(ang).reshape(shape)
    xf = x.astype(jnp.float32)
    x1, x2 = xf[..., :half], xf[..., half:]
    return jnp.concatenate([x1 * cos - x2 * sin, x2 * cos + x1 * sin], axis=-1).astype(x.dtype)


def project(h, pos, w_in, dsa_qn_g, dsa_kn_g, idx_kn_g, mla_cqn_g, mla_ckvn_g, mla_krn_g, mla_qn_g, w_uq, w_uk):
    B, T, _ = h.shape
    z = h @ w_in
    rq, rk, rv, rg, dq, dk, dv, iq, ik, iw, cq, ckv, kr = jnp.split(z, np.cumsum(IN_SPLITS)[:-1], axis=-1)
    rq = rope(rq.reshape(B, T, H_RET, DK_RET), pos)
    rk = rope(rk.reshape(B, T, H_RET, DK_RET), pos) * (DK_RET ** -0.5)
    rv = rv.reshape(B, T, H_RET, DV_RET)
    dq = rope(rmsnorm(dq.reshape(B, T, H_DSA, HD_DSA), dsa_qn_g), pos)
    dk = rope(rmsnorm(dk, dsa_kn_g), pos)
    iq = rope(iq.reshape(B, T, H_IDX, D_IDX), pos)
    ik = rope(rmsnorm(ik, idx_kn_g), pos)
    iw = iw * (H_IDX ** -0.5)
    q = (rmsnorm(cq, mla_cqn_g) @ w_uq).reshape(B, T, H_MLA, D_NOPE + D_ROPE)
    q = rmsnorm(q, mla_qn_g)
    q_lat = jnp.einsum('bthn,hnc->bthc', q[..., :D_NOPE], w_uk)
    q_rope = rope(q[..., D_NOPE:], pos)
    ckv = rmsnorm(ckv, mla_ckvn_g)
    kr = rope(rmsnorm(kr, mla_krn_g), pos)
    dsa_row = jnp.concatenate([dk, dv, ik], axis=-1)
    mla_row = jnp.concatenate([ckv, kr], axis=-1)
    return (rq, rk, rv, rg), (dq, iq, iw), (q_lat, q_rope), dsa_row, mla_row


def retention_chunk(state, q, k, v, log_gamma):
    C = q.shape[1]
    i = jnp.arange(C, dtype=jnp.float32)
    lg = log_gamma[:, None]
    rel = i[:, None] - i[None, :]
    decay = jnp.where(rel >= 0, jnp.exp(log_gamma[:, None, None] * jnp.maximum(rel, 0.0)), 0.0)
    qf, kf, vf = q.astype(jnp.float32), k.astype(jnp.float32), v.astype(jnp.float32)
    attn = jnp.einsum('bihd,bjhd->bhij', qf, kf) * decay[None]
    o = jnp.einsum('bhij,bjhv->bihv', attn, vf)
    o = o + jnp.einsum('bihd,bhdv->bihv', qf, state) * jnp.exp(lg * (i + 1.0)).T[None, :, :, None]
    k_dec = kf * jnp.exp(lg * (C - 1.0 - i)).T[None, :, :, None]
    new_state = state * jnp.exp(log_gamma * C)[None, :, None, None] + jnp.einsum('bjhd,bjhv->bhdv', k_dec, vf)
    return new_state, o


def retention_prompt(q, k, v, log_gamma):
    B, S, H, dk = q.shape
    nc = S // RET_CHUNK
    def chunks(a):
        return a.reshape((B, nc, RET_CHUNK) + a.shape[2:]).swapaxes(0, 1)
    state0 = jnp.zeros((B, H, dk, v.shape[-1]), jnp.float32)
    def step(st, xs):
        return retention_chunk(st, xs[0], xs[1], xs[2], log_gamma)
    st, o = lax.scan(step, state0, (chunks(q), chunks(k), chunks(v)))
    return st, o.swapaxes(0, 1).reshape(B, S, H, v.shape[-1])


def dsa_select(q_idx, w_idx, q_pos, k_idx, topk):
    L = k_idx.shape[0]
    rel = jax.nn.relu(jnp.einsum('thd,sd->ths', q_idx.astype(jnp.float32), k_idx.astype(jnp.float32))) * (D_IDX ** -0.5)
    score = jnp.einsum('ths,th->ts', rel, w_idx.astype(jnp.float32))
    score = jnp.where(jnp.arange(L)[None, :] <= q_pos[:, None], score, -jnp.inf)
    _, idx = lax.top_k(score, topk)
    return idx


def dsa_attend(q, k_sel, v_sel, valid):
    s = jnp.einsum('thd,tkd->thk', q.astype(jnp.float32), k_sel.astype(jnp.float32)) * (HD_DSA ** -0.5)
    p = jax.nn.softmax(jnp.where(valid[:, None, :], s, -jnp.inf), axis=-1)
    return jnp.einsum('thk,tkd->thd', p, v_sel.astype(jnp.float32)).astype(q.dtype)


def mla_attend(q_lat, q_rope, q_pos, c_kv, k_rope, w_uv):
    L = c_kv.shape[0]
    ckf = c_kv.astype(jnp.float32)
    s = (jnp.einsum('thc,sc->hts', q_lat.astype(jnp.float32), ckf)
         + jnp.einsum('thr,sr->hts', q_rope.astype(jnp.float32), k_rope.astype(jnp.float32))) * ((D_NOPE + D_ROPE) ** -0.5)
    mask = jnp.arange(L)[None, :] <= q_pos[:, None]
    p = jax.nn.softmax(jnp.where(mask[None], s, -jnp.inf), axis=-1)
    o_lat = jnp.einsum('hts,sc->thc', p, ckf)
    return jnp.einsum('thc,hcv->thv', o_lat, w_uv.astype(jnp.float32)).astype(q_lat.dtype)


def _blocks(a, nb):
    return a.reshape((a.shape[0] * nb, Q_BLOCK) + a.shape[2:])


def _block_ids(B, nb):
    return jnp.repeat(jnp.arange(B), nb), jnp.tile(jnp.arange(nb) * Q_BLOCK, B)


def dsa_prompt(dq, iq, iw, dsa_row, topk):
    B, S = dq.shape[:2]
    nb = S // Q_BLOCK
    k_all = dsa_row[..., :HD_DSA]
    v_all = dsa_row[..., HD_DSA:2 * HD_DSA]
    ik_all = dsa_row[..., 2 * HD_DSA:]
    b_ids, starts = _block_ids(B, nb)
    def one(xs):
        qb, iqb, iwb, b, start = xs
        pos = start + jnp.arange(Q_BLOCK)
        idx = dsa_select(iqb, iwb, pos, ik_all[b], topk)
        return dsa_attend(qb, k_all[b][idx], v_all[b][idx], idx <= pos[:, None])
    o = lax.map(one, (_blocks(dq, nb), _blocks(iq, nb), _blocks(iw, nb), b_ids, starts))
    return o.reshape(B, S, H_DSA, HD_DSA)


def mla_prompt(q_lat, q_rope, mla_row, w_uv):
    B, S = q_lat.shape[:2]
    nb = S // Q_BLOCK
    ckv = mla_row[..., :D_KV_LAT]
    kr = mla_row[..., D_KV_LAT:]
    b_ids, starts = _block_ids(B, nb)
    def one(xs):
        qlb, qrb, b, start = xs
        return mla_attend(qlb, qrb, start + jnp.arange(Q_BLOCK), ckv[b], kr[b], w_uv)
    o = lax.map(one, (_blocks(q_lat, nb), _blocks(q_rope, nb), b_ids, starts))
    return o.reshape(B, S, H_MLA, DV_MLA)


def sample_attention(l, cache_dsa, cache_mla, page_table, dq, iq, iw, dsa_new, q_lat, q_rope, mla_new, w_uv, topk):
    T = dq.shape[1]
    past = page_table.shape[1] * PAGE_SIZE
    pos = past + jnp.arange(T)
    def one(xs):
        pt, dqb, iqb, iwb, dnew, qlb, qrb, mnew = xs
        ik_past = cache_dsa[l, pt, :, 2 * HD_DSA:].reshape(past, D_IDX)
        idx = dsa_select(iqb, iwb, pos, jnp.concatenate([ik_past, dnew[:, 2 * HD_DSA:]], axis=0), topk)
        pidx = jnp.minimum(idx, past - 1)
        past_rows = cache_dsa[l, pt[pidx // PAGE_SIZE], pidx % PAGE_SIZE, :2 * HD_DSA]
        new_rows = dnew[jnp.clip(idx - past, 0, T - 1), :2 * HD_DSA]
        kv = jnp.where((idx < past)[..., None], past_rows, new_rows)
        o_dsa = dsa_attend(dqb, kv[..., :HD_DSA], kv[..., HD_DSA:], idx <= pos[:, None])
        lat = jnp.concatenate([cache_mla[l, pt].reshape(past, MLA_ROW), mnew], axis=0)
        o_mla = mla_attend(qlb, qrb, pos, lat[:, :D_KV_LAT], lat[:, D_KV_LAT:], w_uv)
        return o_dsa, o_mla
    return lax.map(one, (page_table, dq, iq, iw, dsa_new, q_lat, q_rope, mla_new))


def mix_out(o_ret, rg, o_dsa, o_mla, ret_gn_g, w_out):
    B, T = rg.shape[:2]
    mu = jnp.mean(o_ret, axis=-1, keepdims=True)
    var = jnp.mean(jnp.square(o_ret - mu), axis=-1, keepdims=True)
    o_ret = (o_ret - mu) * lax.rsqrt(var + EPS) * ret_gn_g.astype(jnp.float32)
    o_ret = o_ret.reshape(B, T, -1) * jax.nn.silu(rg.astype(jnp.float32))
    dt = rg.dtype
    cat = jnp.concatenate([o_ret.astype(dt), o_dsa.reshape(B, T, -1).astype(dt), o_mla.reshape(B, T, -1).astype(dt)], axis=-1)
    return cat @ w_out


def hier_moe(h, w_group, b_group, w_router, b_router, w1, w3, w2):
    B, T, _ = h.shape
    g_logit = (h @ w_group + b_group).astype(jnp.float32)
    grp = jnp.argmax(g_logit, axis=-1)
    g_prob = jnp.take_along_axis(jax.nn.softmax(g_logit, axis=-1), grp[..., None], axis=-1)
    e_logit = (h @ w_router + b_router).astype(jnp.float32).reshape(B, T, N_GROUPS, EXPERTS_PER_GROUP)
    e_logit = jnp.take_along_axis(e_logit, grp[..., None, None], axis=2)[..., 0, :]
    top_p, top_i = lax.top_k(jax.nn.softmax(e_logit, axis=-1), TOP_K_IN_GROUP)
    top_p = top_p / jnp.sum(top_p, axis=-1, keepdims=True) * g_prob
    eid = grp[..., None] * EXPERTS_PER_GROUP + top_i
    gate = jnp.sum(jax.nn.one_hot(eid, N_EXPERTS, dtype=jnp.float32) * top_p[..., None], axis=-2)
    a = jnp.einsum('btd,edf->btef', h, w1)
    u = jnp.einsum('btd,edf->btef', h, w3)
    act = (jax.nn.silu(a.astype(jnp.float32)) * u.astype(jnp.float32) * gate[..., None]).astype(h.dtype)
    return jnp.einsum('btef,efd->btd', act, w2)


def setup_inputs(seed: int = 0) -> dict:
    key = jax.random.key(seed)
    ks = iter(jax.random.split(key, 40))
    f32 = jnp.float32
    def nrm(shape, scale):
        return jax.random.normal(next(ks), shape, f32) * scale
    def gain(shape):
        return 1.0 + 0.02 * jax.random.normal(next(ks), shape, f32)
    n_pages = PAST_LEN // PAGE_SIZE
    n_used = DEC_BATCH * n_pages
    n_pool = n_used + (n_used + 3) // 4
    L = DEPTH
    return {
        'x_prompt': nrm((BATCH, SEQ, D_MODEL), 1.0),
        'x_sample': nrm((DEC_BATCH, DEC_SEQ, D_MODEL), 1.0),
        'cache_dsa': nrm((L, n_pool, PAGE_SIZE, DSA_ROW), 1.0),
        'cache_mla': nrm((L, n_pool, PAGE_SIZE, MLA_ROW), 1.0),
        'state_ret': nrm((L, DEC_BATCH, H_RET, DK_RET, DV_RET), 1.0),
        'page_table': jax.random.permutation(next(ks), n_pool)[:n_used].reshape(DEC_BATCH, n_pages).astype(jnp.int32),
        'c_prompt': nrm((BATCH, D_MODEL), 1.0),
        'c_sample': nrm((DEC_BATCH, D_MODEL), 1.0),
        'norm1_g': gain((L, D_MODEL)),
        'norm2_g': gain((L, D_MODEL)),
        'w_ada': nrm((L, D_MODEL, 6 * D_MODEL), 0.5 * D_MODEL ** -0.5),
        'b_ada': nrm((L, 6 * D_MODEL), 0.01),
        'w_in': nrm((L, D_MODEL, D_IN), D_MODEL ** -0.5),
        'ret_gn_g': gain((L, H_RET, DV_RET)),
        'dsa_qn_g': gain((L, HD_DSA)),
        'dsa_kn_g': gain((L, HD_DSA)),
        'idx_kn_g': gain((L, D_IDX)),
        'mla_cqn_g': gain((L, D_Q_LAT)),
        'mla_ckvn_g': gain((L, D_KV_LAT)),
        'mla_krn_g': gain((L, D_ROPE)),
        'mla_qn_g': gain((L, D_NOPE + D_ROPE)),
        'w_uq': nrm((L, D_Q_LAT, H_MLA * (D_NOPE + D_ROPE)), D_Q_LAT ** -0.5),
        'w_uk': nrm((L, H_MLA, D_NOPE, D_KV_LAT), D_KV_LAT ** -0.5),
        'w_uv': nrm((L, H_MLA, D_KV_LAT, DV_MLA), D_KV_LAT ** -0.5),
        'w_out': nrm((L, D_MIX, D_MODEL), D_MIX ** -0.5),
        'w_group': nrm((L, D_MODEL, N_GROUPS), D_MODEL ** -0.5),
        'b_group': nrm((L, N_GROUPS), 0.01),
        'w_router': nrm((L, D_MODEL, N_EXPERTS), D_MODEL ** -0.5),
        'b_router': nrm((L, N_EXPERTS), 0.01),
        'w1': nrm((L, N_EXPERTS, D_MODEL, D_EXPERT), D_MODEL ** -0.5),
        'w3': nrm((L, N_EXPERTS, D_MODEL, D_EXPERT), D_MODEL ** -0.5),
        'w2': nrm((L, N_EXPERTS, D_EXPERT, D_MODEL), D_EXPERT ** -0.5),
    }


def reference(x_prompt, x_sample, cache_dsa, cache_mla, state_ret, page_table, c_prompt, c_sample,
              norm1_g, norm2_g, w_ada, b_ada, w_in, ret_gn_g, dsa_qn_g, dsa_kn_g, idx_kn_g,
              mla_cqn_g, mla_ckvn_g, mla_krn_g, mla_qn_g, w_uq, w_uk, w_uv, w_out,
              w_group, b_group, w_router, b_router, w1, w3, w2):
    S = x_prompt.shape[1]
    T = x_sample.shape[1]
    past = page_table.shape[1] * PAGE_SIZE
    pos_p = jnp.arange(S)
    pos_s = past + jnp.arange(T)
    topk_p = min(TOPK_MAX, S // 4)
    topk_s = min(TOPK_MAX, (past + T) // 4)
    log_gamma = jnp.log(1.0 - 2.0 ** (-5.0 - jnp.arange(H_RET, dtype=jnp.float32)))
    xp, xs = x_prompt, x_sample
    dsa_p, dsa_s, mla_p, mla_s, ret_p, ret_s = [], [], [], [], [], []
    for l in range(DEPTH):
        mp = jnp.split(jax.nn.silu(c_prompt) @ w_ada[l] + b_ada[l], 6, axis=-1)
        ms = jnp.split(jax.nn.silu(c_sample) @ w_ada[l] + b_ada[l], 6, axis=-1)
        pw = (w_in[l], dsa_qn_g[l], dsa_kn_g[l], idx_kn_g[l], mla_cqn_g[l], mla_ckvn_g[l], mla_krn_g[l], mla_qn_g[l], w_uq[l], w_uk[l])
        moe_w = (w_group[l], b_group[l], w_router[l], b_router[l], w1[l], w3[l], w2[l])
        hp = modulate(xp, norm1_g[l], mp[0], mp[1])
        (rq, rk, rv, rg), (dq, iq, iw), (ql, qr), drow, mrow = project(hp, pos_p, *pw)
        st_p, o_ret = retention_prompt(rq, rk, rv, log_gamma)
        o_dsa = dsa_prompt(dq, iq, iw, drow, topk_p)
        o_mla = mla_prompt(ql, qr, mrow, w_uv[l])
        xp = xp + mp[2][:, None, :] * mix_out(o_ret, rg, o_dsa, o_mla, ret_gn_g[l], w_out[l])
        xp = xp + mp[5][:, None, :] * hier_moe(modulate(xp, norm2_g[l], mp[3], mp[4]), *moe_w)
        dsa_p.append(drow)
        mla_p.append(mrow)
        ret_p.append(st_p)
        hs = modulate(xs, norm1_g[l], ms[0], ms[1])
        (rq, rk, rv, rg), (dq, iq, iw), (ql, qr), drow, mrow = project(hs, pos_s, *pw)
        st_s, o_ret = retention_chunk(state_ret[l].astype(jnp.float32), rq, rk, rv, log_gamma)
        o_dsa, o_mla = sample_attention(l, cache_dsa, cache_mla, page_table, dq, iq, iw, drow, ql, qr, mrow, w_uv[l], topk_s)
        xs = xs + ms[2][:, None, :] * mix_out(o_ret, rg, o_dsa, o_mla, ret_gn_g[l], w_out[l])
        xs = xs + ms[5][:, None, :] * hier_moe(modulate(xs, norm2_g[l], ms[3], ms[4]), *moe_w)
        dsa_s.append(drow)
        mla_s.append(mrow)
        ret_s.append(st_s)
    return (xp, xs, jnp.stack(dsa_p), jnp.stack(dsa_s), jnp.stack(mla_p), jnp.stack(mla_s), jnp.stack(ret_p), jnp.stack(ret_s))
```

```python
import functools
import math

import numpy as np
import jax
import jax.numpy as jnp
from jax import lax
from jax.experimental import pallas as pl
from jax.experimental.pallas import tpu as pltpu

F32 = jnp.float32
BF16 = jnp.bfloat16
I32 = jnp.int32

D_MODEL = 1024
PAGE = 128
H_RET, DK_RET, DV_RET = 6, 64, 64
H_DSA, HD_DSA = 4, 64
H_IDX, D_IDX = 8, 64
TOPK_MAX = 256
H_MLA, D_Q_LAT, D_KV_LAT, D_NOPE, D_ROPE, DV_MLA = 6, 256, 128, 64, 32, 64
N_GROUPS, EPG, N_EXPERTS, D_EXPERT = 4, 8, 32, 256
Q_BLOCK = 128
ROPE_THETA = 10000.0
EPS = 1e-6
DSA_ROW = 2 * HD_DSA + D_IDX
MLA_ROW = D_KV_LAT + D_ROPE
D_QK_MLA = D_NOPE + D_ROPE
IN_SPLITS = (384, 384, 384, 384, 256, 64, 64, 512, 64, 8, 256, 128, 32)

C_RQ, C_RK, C_RV, C_RG = 0, 384, 768, 1152
C_DQ, C_KV, C_IQ, C_IK, C_CQ, C_CKV, C_KR, C_IW = 1536, 1792, 1920, 2432, 2560, 2816, 2944, 3072
D_PACK = 3200
MLA_QW = 640

LANES = 128
VMEM_LIMIT = 56 * 1024 * 1024

INT_MIN = -2 ** 31
NEG_BIG = -1e30
KEY_NEG_INF = (0xFF800000 ^ 0x7FFFFFFF) - 2 ** 32


def _cparams(sem):
    return pltpu.CompilerParams(dimension_semantics=sem, vmem_limit_bytes=VMEM_LIMIT)


def _dot(a, b):
    return jnp.dot(a, b, preferred_element_type=F32)


def _dot_nt(a, b):
    return lax.dot_general(a, b, (((1,), (1,)), ((), ())), preferred_element_type=F32)


def _split(x):
    hi = x.astype(BF16)
    lo = (x - hi.astype(F32)).astype(BF16)
    return hi, lo


def _dot_f32_lhs(x, b_bf16):
    hi, lo = _split(x)
    return _dot(hi, b_bf16) + _dot(lo, b_bf16)


def _sigmoid(x):
    return 1.0 / (1.0 + jnp.exp(-x))


def _sort_key(x):
    b = lax.bitcast_convert_type(x, I32)
    return b ^ ((b >> 31) & 0x7FFFFFFF)


def _rope(x, cos, sin, half):
    w = x.shape[-1]
    fwd = pltpu.roll(x, w - half, axis=1)
    bwd = pltpu.roll(x, half, axis=1)
    lane = lax.broadcasted_iota(I32, x.shape, 1)
    first = (lane & (2 * half - 1)) < half
    return x * cos + jnp.where(first, fwd, bwd) * sin


def _tile_lanes(t, n):
    return t if n == 1 else jnp.concatenate([t] * n, axis=1)


def _ada_kernel(c_ref, w_ref, b_ref, o_ref):
    c = c_ref[...]
    a = c * _sigmoid(c)
    ah, al = _split(a)
    w = w_ref[0]
    wh, wl = _split(w)
    o_ref[0] = _dot(ah, wh) + _dot(ah, wl) + _dot(al, wh) + b_ref[0]


def _ada(c_all, w_ada, b_ada):
    L = w_ada.shape[0]
    R = c_all.shape[0]
    tn = 1024
    return pl.pallas_call(
        _ada_kernel,
        grid=(L, 6 * D_MODEL // tn),
        in_specs=[pl.BlockSpec((R, D_MODEL), lambda l, j: (0, 0)),
                  pl.BlockSpec((1, D_MODEL, tn), lambda l, j: (l, 0, j)),
                  pl.BlockSpec((1, 1, tn), lambda l, j: (l, 0, j))],
        out_specs=pl.BlockSpec((1, R, tn), lambda l, j: (l, 0, j)),
        out_shape=jax.ShapeDtypeStruct((L, R, 6 * D_MODEL), F32),
        compiler_params=_cparams(("arbitrary", "arbitrary")),
    )(c_all, w_ada, b_ada.reshape(L, 1, 6 * D_MODEL))


def _proj_kernel(x_ref, sh_ref, sc_ref, g1_ref, c64_ref, s64_ref, c32_ref, s32_ref,
                 w_ref, gq_ref, gkv_ref, gik_ref, gcq_ref, gckv_ref, gkr_ref, gmq_ref,
                 wuq_ref, wuk_ref, b64_ref, bq_ref,
                 rq_ref, rk_ref, rv_ref, rg_ref, dq_ref, iq_ref, iw_ref, iwt_ref,
                 drow_ref, kd_ref, ikd_ref, kvt_ref, qc_ref, mrow_ref, kvm_ref, ckvt_ref):
    x = x_ref[...]
    rstd = lax.rsqrt(jnp.mean(x * x, axis=-1, keepdims=True) + EPS)
    h = (x * rstd) * g1_ref[...] * (1.0 + sc_ref[...]) + sh_ref[...]
    hb = h.astype(BF16)
    c64, s64, c32, s32 = c64_ref[...], s64_ref[...], c32_ref[...], s32_ref[...]

    def zcols(c0, width):
        return _dot(hb, w_ref[:, c0:c0 + width])

    rq_ref[...] = _rope(zcols(C_RQ, 384), _tile_lanes(c64, 3), _tile_lanes(s64, 3), 32)
    rk_ref[...] = _rope(zcols(C_RK, 384), _tile_lanes(c64, 3), _tile_lanes(s64, 3), 32) * (DK_RET ** -0.5)
    rv_ref[...] = zcols(C_RV, 384)
    rg_ref[...] = zcols(C_RG, 384)

    dq = zcols(C_DQ, 256)
    ss = _dot_f32_lhs(dq * dq, b64_ref[...])
    dq = dq * lax.rsqrt(ss * (1.0 / HD_DSA) + EPS) * gq_ref[...]
    dq = _rope(dq, _tile_lanes(c64, 2), _tile_lanes(s64, 2), 32) * (HD_DSA ** -0.5)
    for hh in range(H_DSA):
        dq_ref[hh] = dq[:, 64 * hh:64 * hh + 64].astype(dq_ref.dtype)

    kv = zcols(C_KV, 128)
    lane = lax.broadcasted_iota(I32, kv.shape, 1)
    is_k = lane < HD_DSA
    ssk = jnp.sum(jnp.where(is_k, kv * kv, 0.0), axis=-1, keepdims=True)
    kn = kv * lax.rsqrt(ssk * (1.0 / HD_DSA) + EPS) * gkv_ref[...]
    kv = jnp.where(is_k, _rope(kn, c64, s64, 32), kv)
    drow_ref[:, 0:128] = kv
    kvb = kv.astype(BF16)
    kd_ref[...] = kvb[:, 0:64]
    kvt_ref[0] = kv.T.astype(BF16)

    iq = _rope(zcols(C_IQ, 512), _tile_lanes(c64, 4), _tile_lanes(s64, 4), 32) * (D_IDX ** -0.5)
    for hh in range(H_IDX):
        iq_ref[hh] = iq[:, 64 * hh:64 * hh + 64].astype(iq_ref.dtype)
    ik = zcols(C_IK, 128)
    ssi = jnp.sum(ik * ik, axis=-1, keepdims=True)
    ik = _rope(ik * lax.rsqrt(ssi * (1.0 / D_IDX) + EPS) * gik_ref[...], c64, s64, 32)
    drow_ref[:, 128:192] = ik[:, 0:64]
    ikd_ref[...] = ik[:, 0:64].astype(BF16)
    iw = zcols(C_IW, 128) * (H_IDX ** -0.5)
    iw_ref[...] = iw
    iwt_ref[...] = iw.T[0:H_IDX, :]

    cq = zcols(C_CQ, 256)
    cq = cq * lax.rsqrt(jnp.mean(cq * cq, axis=-1, keepdims=True) + EPS) * gcq_ref[...]
    q = _dot(cq.astype(BF16), wuq_ref[...])
    ssq = _dot_f32_lhs(q * q, bq_ref[...])
    q = q * lax.rsqrt(ssq * (1.0 / D_QK_MLA) + EPS) * gmq_ref[...]
    scale = D_QK_MLA ** -0.5
    q_lat = _dot(q[:, 0:384].astype(BF16), wuk_ref[...]) * scale
    q_rope = _rope(q[:, 384:640], _tile_lanes(c32, 2), _tile_lanes(s32, 2), 16) * scale
    for hh in range(H_MLA):
        qc_ref[hh, :, 0:128] = q_lat[:, 128 * hh:128 * hh + 128].astype(qc_ref.dtype)
        qc_ref[hh, :, 128:160] = q_rope[:, 32 * hh:32 * hh + 32].astype(qc_ref.dtype)
    ckv = zcols(C_CKV, 128)
    ckv = ckv * lax.rsqrt(jnp.mean(ckv * ckv, axis=-1, keepdims=True) + EPS) * gckv_ref[...]
    kr = zcols(C_KR, 128)
    ssr = jnp.sum(kr * kr, axis=-1, keepdims=True)
    kr = _rope(kr * lax.rsqrt(ssr * (1.0 / D_ROPE) + EPS) * gkr_ref[...], c32, s32, 16)
    mrow_ref[:, 0:128] = ckv
    mrow_ref[:, 128:160] = kr[:, 0:32]
    kvm_ref[:, 0:128] = ckv.astype(BF16)
    kvm_ref[:, 128:160] = kr[:, 0:32].astype(BF16)
    ckvt_ref[0] = ckv.T.astype(BF16)


def _proj(x, shift, scale, g1, tabs, lw, qdtype, tm):
    T = x.shape[0]
    nb = T // tm
    per_tok = shift.shape[0] != 1

    def row(width):
        return pl.BlockSpec((tm, width), lambda i: (i, 0))

    def const(shape):
        nd = len(shape)
        return pl.BlockSpec(shape, lambda i: (0,) * nd)

    mod_spec = row(D_MODEL) if per_tok else const((1, D_MODEL))
    in_specs = [row(D_MODEL), mod_spec, mod_spec, const((1, D_MODEL)),
                row(LANES), row(LANES), row(LANES), row(LANES),
                const((D_MODEL, D_PACK)),
                const((1, 256)), const((1, 128)), const((1, 128)), const((1, 256)),
                const((1, 128)), const((1, 128)), const((1, MLA_QW)),
                const((D_Q_LAT, MLA_QW)), const((384, 768)), const((256, 256)), const((MLA_QW, MLA_QW))]
    out_shape = [
        jax.ShapeDtypeStruct((T, 384), F32), jax.ShapeDtypeStruct((T, 384), F32),
        jax.ShapeDtypeStruct((T, 384), F32), jax.ShapeDtypeStruct((T, 384), F32),
        jax.ShapeDtypeStruct((H_DSA, T, 64), qdtype), jax.ShapeDtypeStruct((H_IDX, T, 64), qdtype),
        jax.ShapeDtypeStruct((T, LANES), F32), jax.ShapeDtypeStruct((H_IDX, T), F32),
        jax.ShapeDtypeStruct((T, DSA_ROW), F32), jax.ShapeDtypeStruct((T, 64), BF16),
        jax.ShapeDtypeStruct((T, 64), BF16), jax.ShapeDtypeStruct((nb, 128, tm), BF16),
        jax.ShapeDtypeStruct((H_MLA, T, 160), qdtype), jax.ShapeDtypeStruct((T, MLA_ROW), F32),
        jax.ShapeDtypeStruct((T, MLA_ROW), BF16), jax.ShapeDtypeStruct((nb, 128, tm), BF16)]
    out_specs = [
        row(384), row(384), row(384), row(384),
        pl.BlockSpec((H_DSA, tm, 64), lambda i: (0, i, 0)), pl.BlockSpec((H_IDX, tm, 64), lambda i: (0, i, 0)),
        row(LANES), pl.BlockSpec((H_IDX, tm), lambda i: (0, i)),
        row(DSA_ROW), row(64), row(64), pl.BlockSpec((1, 128, tm), lambda i: (i, 0, 0)),
        pl.BlockSpec((H_MLA, tm, 160), lambda i: (0, i, 0)), row(MLA_ROW), row(MLA_ROW),
        pl.BlockSpec((1, 128, tm), lambda i: (i, 0, 0))]
    return pl.pallas_call(
        _proj_kernel, grid=(nb,), in_specs=in_specs, out_specs=out_specs, out_shape=out_shape,
        compiler_params=_cparams(("arbitrary",)),
    )(x, shift, scale, g1, *tabs, lw["w_pack"], lw["g_dq"], lw["g_kv"], lw["g_ik"], lw["g_cq"],
      lw["g_ckv"], lw["g_kr"], lw["g_mq"], lw["w_uq"], lw["w_ukbd"], lw["b64"], lw["bq"])


def _ret_kernel(log_gammas, C, q_ref, k_ref, v_ref, g_ref, gn_ref, s0_ref, o_ref, st_ref):
    first = pl.program_id(1) == 0

    @pl.when(first)
    def _():
        st_ref[...] = s0_ref[...]

    ri = lax.broadcasted_iota(I32, (C, C), 0)
    ci = lax.broadcasted_iota(I32, (C, C), 1)
    rel = (ri - ci).astype(F32)
    pos = lax.broadcasted_iota(I32, (C, 1), 0).astype(F32)
    q_all, k_all, v_all, g_all, gn = q_ref[...], k_ref[...], v_ref[...], g_ref[...], gn_ref[...]
    for hh in range(H_RET):
        lg = log_gammas[hh]
        sl = slice(64 * hh, 64 * hh + 64)
        q, k, v = q_all[:, sl], k_all[:, sl], v_all[:, sl]
        qb, vb = q.astype(BF16), v.astype(BF16)
        decay = jnp.where(rel >= 0, jnp.exp(lg * jnp.maximum(rel, 0.0)), 0.0)
        attn = _dot_nt(qb, k.astype(BF16)) * decay
        state = st_ref[0, hh]
        o = _dot(attn.astype(BF16), vb) + _dot(qb, state.astype(BF16)) * jnp.exp(lg * (pos + 1.0))
        k_dec = k * jnp.exp(lg * (C - 1.0 - pos))
        st_ref[0, hh] = state * math.exp(lg * C) + _dot(k_dec.T.astype(BF16), vb)
        mu = jnp.mean(o, axis=-1, keepdims=True)
        d = o - mu
        var = jnp.mean(d * d, axis=-1, keepdims=True)
        gate = g_all[:, sl]
        o_ref[:, sl] = (d * lax.rsqrt(var + EPS) * gn[:, sl] * (gate * _sigmoid(gate))).astype(o_ref.dtype)


def _retention(rq, rk, rv, rg, gn, state0, C, log_gammas, out_dtype):
    NB = state0.shape[0]
    T = rq.shape[0]
    nc = T // (NB * C)
    row = pl.BlockSpec((C, 384), lambda b, c: (b * nc + c, 0))
    st = pl.BlockSpec((1, H_RET, DK_RET, DV_RET), lambda b, c: (b, 0, 0, 0))
    return pl.pallas_call(
        functools.partial(_ret_kernel, log_gammas, C),
        grid=(NB, nc),
        in_specs=[row, row, row, row, pl.BlockSpec((1, 384), lambda b, c: (0, 0)), st],
        out_specs=[row, st],
        out_shape=[jax.ShapeDtypeStruct((T, 384), out_dtype),
                   jax.ShapeDtypeStruct((NB, H_RET, DK_RET, DV_RET), F32)],
        compiler_params=_cparams(("arbitrary", "arbitrary")),
    )(rq, rk, rv, rg, gn, state0)


def _count_ge(keys_ref, nch, kc, cand):
    def body(c, acc):
        blk = keys_ref[pl.ds(pl.multiple_of(c * kc, kc), kc), :]
        ind = jnp.where(blk >= cand, 1, 0).astype(I32)
        return acc + jnp.sum(ind.reshape(kc // 8, 8, LANES), axis=0)
    acc = lax.fori_loop(0, nch, body, jnp.zeros((8, LANES), I32))
    return jnp.sum(acc, axis=0, keepdims=True)


def _kth_largest_key(count_fn, topk, shape):
    def body(it, t_u):
        bit = 31 - it
        cand_u = t_u | (jnp.int32(1) << bit)
        cnt = count_fn(cand_u ^ INT_MIN)
        return jnp.where(cnt >= topk, cand_u, t_u)
    t_u = lax.fori_loop(0, 32, body, jnp.zeros(shape, I32))
    return t_u ^ INT_MIN


def _dsa_prompt_kernel(topk, kc, iq_ref, iwt_ref, dq_ref, ikd_ref, kd_ref, kvt_ref, o_ref, keys_ref):
    i = pl.program_id(0)
    nch = (i * Q_BLOCK + Q_BLOCK + kc - 1) // kc
    q_pos = i * Q_BLOCK + lax.broadcasted_iota(I32, (kc, LANES), 1)
    row_iota = lax.broadcasted_iota(I32, (kc, LANES), 0)
    iqs = iq_ref[...].reshape(H_IDX * Q_BLOCK, D_IDX)
    iwt = iwt_ref[...]

    def idx_body(c, carry):
        start = pl.multiple_of(c * kc, kc)
        s = _dot_nt(ikd_ref[pl.ds(start, kc), :], iqs)
        score = jnp.zeros((kc, LANES), F32)
        for hh in range(H_IDX):
            score = score + jnp.maximum(s[:, LANES * hh:LANES * hh + LANES], 0.0) * iwt[hh:hh + 1, :]
        score = jnp.where(start + row_iota <= q_pos, score, -jnp.inf)
        keys_ref[pl.ds(start, kc), :] = _sort_key(score)
        return carry
    lax.fori_loop(0, nch, idx_body, 0)

    thr = _kth_largest_key(lambda cand: _count_ge(keys_ref, nch, kc, cand), topk, (1, LANES))
    thr4 = _tile_lanes(jnp.maximum(thr, KEY_NEG_INF + 1), H_DSA)

    dqs = dq_ref[...].reshape(H_DSA * Q_BLOCK, HD_DSA)
    W = H_DSA * LANES

    def att_body(c, carry):
        m, l, acc = carry
        start = pl.multiple_of(c * kc, kc)
        s = _dot_nt(kd_ref[pl.ds(start, kc), :], dqs)
        k4 = _tile_lanes(keys_ref[pl.ds(start, kc), :], H_DSA)
        s = jnp.where(k4 >= thr4, s, NEG_BIG)
        m_new = jnp.maximum(m, jnp.max(s, axis=0, keepdims=True))
        alpha = jnp.exp(m - m_new)
        p = jnp.exp(s - m_new)
        l = alpha * l + jnp.sum(p, axis=0, keepdims=True)
        acc = alpha * acc + _dot(kvt_ref[c], p.astype(BF16))
        return m_new, l, acc
    m0 = jnp.full((1, W), NEG_BIG, F32)
    m, l, acc = lax.fori_loop(0, nch, att_body, (m0, jnp.zeros((1, W), F32), jnp.zeros((LANES, W), F32)))
    o_t = acc[HD_DSA:, :] / l
    o_hq = jnp.concatenate([o_t[:, LANES * hh:LANES * hh + LANES] for hh in range(H_DSA)], axis=0)
    o_ref[...] = o_hq.T.astype(o_ref.dtype)


def _dsa_prompt(iq_hm, iwt, dq_hm, ikd, kd, kvt, topk, kc):
    T = ikd.shape[0]
    nq = T // Q_BLOCK
    nkb = kvt.shape[0]
    return pl.pallas_call(
        functools.partial(_dsa_prompt_kernel, topk, kc),
        grid=(nq,),
        in_specs=[pl.BlockSpec((H_IDX, Q_BLOCK, D_IDX), lambda i: (0, i, 0)),
                  pl.BlockSpec((H_IDX, Q_BLOCK), lambda i: (0, i)),
                  pl.BlockSpec((H_DSA, Q_BLOCK, HD_DSA), lambda i: (0, i, 0)),
                  pl.BlockSpec((T, D_IDX), lambda i: (0, 0)),
                  pl.BlockSpec((T, HD_DSA), lambda i: (0, 0)),
                  pl.BlockSpec((nkb, 128, kc), lambda i: (0, 0, 0))],
        out_specs=pl.BlockSpec((Q_BLOCK, H_DSA * HD_DSA), lambda i: (i, 0)),
        out_shape=jax.ShapeDtypeStruct((T, H_DSA * HD_DSA), BF16),
        scratch_shapes=[pltpu.VMEM((nkb * kc, LANES), I32)],
        compiler_params=_cparams(("arbitrary",)),
    )(iq_hm, iwt, dq_hm, ikd, kd, kvt)


def _mla_prompt_kernel(kc, qc_ref, kvm_ref, ckvt_ref, wuvt_ref, o_ref):
    i = pl.program_id(0)
    nch = (i * Q_BLOCK + Q_BLOCK + kc - 1) // kc
    W = H_MLA * LANES
    qs = qc_ref[...].reshape(H_MLA * Q_BLOCK, MLA_ROW)
    q_pos = i * Q_BLOCK + (lax.broadcasted_iota(I32, (kc, W), 1) & (LANES - 1))
    row_iota = lax.broadcasted_iota(I32, (kc, W), 0)

    def body(c, carry):
        m, l, acc = carry
        start = pl.multiple_of(c * kc, kc)
        s = _dot_nt(kvm_ref[pl.ds(start, kc), :], qs)
        s = jnp.where(start + row_iota <= q_pos, s, NEG_BIG)
        m_new = jnp.maximum(m, jnp.max(s, axis=0, keepdims=True))
        alpha = jnp.exp(m - m_new)
        p = jnp.exp(s - m_new)
        l = alpha * l + jnp.sum(p, axis=0, keepdims=True)
        acc = alpha * acc + _dot(ckvt_ref[c], p.astype(BF16))
        return m_new, l, acc
    m0 = jnp.full((1, W), NEG_BIG, F32)
    m, l, acc = lax.fori_loop(0, nch, body, (m0, jnp.zeros((1, W), F32), jnp.zeros((D_KV_LAT, W), F32)))
    o_lat = (acc / l).astype(BF16)
    outs = [_dot(wuvt_ref[hh], o_lat[:, LANES * hh:LANES * hh + LANES]) for hh in range(H_MLA)]
    o_ref[...] = jnp.concatenate(outs, axis=0).T.astype(o_ref.dtype)


def _mla_prompt(qc_hm, kvm, ckvt, wuvt, kc):
    T = kvm.shape[0]
    nq = T // Q_BLOCK
    nkb = ckvt.shape[0]
    return pl.pallas_call(
        functools.partial(_mla_prompt_kernel, kc),
        grid=(nq,),
        in_specs=[pl.BlockSpec((H_MLA, Q_BLOCK, MLA_ROW), lambda i: (0, i, 0)),
                  pl.BlockSpec((T, MLA_ROW), lambda i: (0, 0)),
                  pl.BlockSpec((nkb, 128, kc), lambda i: (0, 0, 0)),
                  pl.BlockSpec((H_MLA, DV_MLA, D_KV_LAT), lambda i: (0, 0, 0))],
        out_specs=pl.BlockSpec((Q_BLOCK, H_MLA * DV_MLA), lambda i: (i, 0)),
        out_shape=jax.ShapeDtypeStruct((T, H_MLA * DV_MLA), BF16),
        compiler_params=_cparams(("arbitrary",)),
    )(qc_hm, kvm, ckvt, wuvt)


def _dsa_sample_kernel(topk, NP, NJ, T, *refs):
    pt_ref = refs[0]
    pages = refs[1:1 + NP]
    iq_ref, iw_ref, dq_ref, new_ref, o_ref, rows_ref, newp_ref, keys_ref, satt_ref = refs[1 + NP:]
    del pt_ref
    j = pl.program_id(1)
    CH = NP * PAGE
    past = NJ * CH
    iqs = iq_ref[...].reshape(H_IDX * T, D_IDX).astype(BF16)
    dqs = dq_ref[...].reshape(H_DSA * T, HD_DSA)
    dqs = jnp.concatenate([dqs, jnp.zeros_like(dqs)], axis=1).astype(BF16)
    iw = iw_ref[...]

    def idx_scores(rows_b):
        s = _dot_nt(iqs, rows_b[:, 2 * HD_DSA:DSA_ROW])
        score = jnp.zeros((T, rows_b.shape[0]), F32)
        for hh in range(H_IDX):
            score = score + jnp.maximum(s[T * hh:T * hh + T, :], 0.0) * iw[:, hh:hh + 1]
        return score

    base = pl.multiple_of(j * CH, CH)
    for p in range(NP):
        rows_ref[pl.ds(base + p * PAGE, PAGE), :] = pages[p][...].astype(BF16)
    chunk = rows_ref[pl.ds(base, CH), :]
    keys_ref[j] = _sort_key(idx_scores(chunk))
    satt_ref[j] = _dot_nt(dqs, chunk[:, 0:2 * HD_DSA])

    @pl.when(j == NJ - 1)
    def _():
        newp_ref[...] = jnp.zeros((PAGE, DSA_ROW), BF16)
        newp_ref[0:T, :] = new_ref[...].astype(BF16)
        newp = newp_ref[...]
        qi = lax.broadcasted_iota(I32, (T, PAGE), 0)
        kj = lax.broadcasted_iota(I32, (T, PAGE), 1)
        keys_new = _sort_key(jnp.where(kj <= qi, idx_scores(newp), -jnp.inf))
        s_new = _dot_nt(dqs, newp[:, 0:2 * HD_DSA])

        def count_fn(cand):
            def body(c, acc):
                ind = jnp.where(keys_ref[c] >= cand, 1, 0).astype(I32)
                part = ind[:, 0:LANES]
                for g in range(1, CH // LANES):
                    part = part + ind[:, LANES * g:LANES * g + LANES]
                return acc + part
            acc = lax.fori_loop(0, NJ, body, jnp.where(keys_new >= cand, 1, 0).astype(I32))
            return jnp.sum(acc, axis=1, keepdims=True)
        thr = _kth_largest_key(count_fn, topk, (T, 1))
        thr = jnp.maximum(thr, KEY_NEG_INF + 1)
        thr4 = jnp.concatenate([thr] * H_DSA, axis=0)

        def masked(c):
            k4 = jnp.concatenate([keys_ref[c]] * H_DSA, axis=0)
            return jnp.where(k4 >= thr4, satt_ref[c], NEG_BIG)
        sn = jnp.where(jnp.concatenate([keys_new] * H_DSA, axis=0) >= thr4, s_new, NEG_BIG)
        m = lax.fori_loop(0, NJ, lambda c, mm: jnp.maximum(mm, jnp.max(masked(c), axis=1, keepdims=True)),
                          jnp.max(sn, axis=1, keepdims=True))
        pn = jnp.exp(sn - m)

        def pv_body(c, carry):
            l, acc = carry
            p = jnp.exp(masked(c) - m)
            rows_c = rows_ref[pl.ds(pl.multiple_of(c * CH, CH), CH), :]
            return l + jnp.sum(p, axis=1, keepdims=True), acc + _dot(p.astype(BF16), rows_c[:, 0:2 * HD_DSA])
        l, acc = lax.fori_loop(0, NJ, pv_body, (jnp.sum(pn, axis=1, keepdims=True),
                                                _dot(pn.astype(BF16), newp[:, 0:2 * HD_DSA])))
        o = acc / l
        for hh in range(H_DSA):
            o_ref[:, 64 * hh:64 * hh + 64] = o[T * hh:T * hh + T, HD_DSA:2 * HD_DSA]


def _dsa_sample(l, cache, pt_flat, n_pages, iq_hm, iw, dq_hm, new_rows, T, topk):
    B = pt_flat.shape[0] // n_pages
    NP = min(16, n_pages)
    NJ = n_pages // NP
    CH = NP * PAGE

    def page_spec(p):
        return pl.BlockSpec((None, None, PAGE, DSA_ROW),
                            lambda b, j, pt: (l, pt[b * n_pages + j * NP + p], 0, 0))
    in_specs = [page_spec(p) for p in range(NP)] + [
        pl.BlockSpec((H_IDX, T, D_IDX), lambda b, j, pt: (0, b, 0)),
        pl.BlockSpec((T, LANES), lambda b, j, pt: (b, 0)),
        pl.BlockSpec((H_DSA, T, HD_DSA), lambda b, j, pt: (0, b, 0)),
        pl.BlockSpec((T, DSA_ROW), lambda b, j, pt: (b, 0))]
    grid_spec = pltpu.PrefetchScalarGridSpec(
        num_scalar_prefetch=1, grid=(B, NJ), in_specs=in_specs,
        out_specs=pl.BlockSpec((T, H_DSA * HD_DSA), lambda b, j, pt: (b, 0)),
        scratch_shapes=[pltpu.VMEM((n_pages * PAGE, DSA_ROW), BF16), pltpu.VMEM((PAGE, DSA_ROW), BF16),
                        pltpu.VMEM((NJ, T, CH), I32), pltpu.VMEM((NJ, H_DSA * T, CH), F32)])
    return pl.pallas_call(
        functools.partial(_dsa_sample_kernel, topk, NP, NJ, T),
        grid_spec=grid_spec,
        out_shape=jax.ShapeDtypeStruct((B * T, H_DSA * HD_DSA), F32),
        compiler_params=_cparams(("arbitrary", "arbitrary")),
    )(pt_flat, *([cache] * NP), iq_hm, iw, dq_hm, new_rows)


def _mla_sample_kernel(NP, NJ, T, *refs):
    pages = refs[1:1 + NP]
    qc_ref, new_ref, wuv_ref, o_ref, rows_ref, m_ref, l_ref, acc_ref = refs[1 + NP:]
    j = pl.program_id(1)
    R = H_MLA * T
    qs = qc_ref[...].reshape(R, MLA_ROW).astype(BF16)

    @pl.when(j == 0)
    def _():
        m_ref[...] = jnp.full((R, 1), NEG_BIG, F32)
        l_ref[...] = jnp.zeros((R, 1), F32)
        acc_ref[...] = jnp.zeros((R, MLA_ROW), F32)

    def update(rows_b, s):
        m_old = m_ref[...]
        m_new = jnp.maximum(m_old, jnp.max(s, axis=1, keepdims=True))
        alpha = jnp.exp(m_old - m_new)
        p = jnp.exp(s - m_new)
        l_ref[...] = alpha * l_ref[...] + jnp.sum(p, axis=1, keepdims=True)
        acc_ref[...] = alpha * acc_ref[...] + _dot(p.astype(BF16), rows_b)
        m_ref[...] = m_new

    for p in range(NP):
        rows_ref[p * PAGE:(p + 1) * PAGE, :] = pages[p][...].astype(BF16)
    rows_b = rows_ref[...]
    update(rows_b, _dot_nt(qs, rows_b))

    @pl.when(j == NJ - 1)
    def _():
        rows_ref[0:PAGE, :] = jnp.zeros((PAGE, MLA_ROW), BF16)
        rows_ref[0:T, :] = new_ref[...].astype(BF16)
        newp = rows_ref[0:PAGE, :]
        qi = lax.broadcasted_iota(I32, (R, PAGE), 0) & (T - 1)
        kj = lax.broadcasted_iota(I32, (R, PAGE), 1)
        update(newp, jnp.where(kj <= qi, _dot_nt(qs, newp), NEG_BIG))
        o_lat = (acc_ref[...][:, 0:D_KV_LAT] / l_ref[...]).astype(BF16)
        o_all = _dot(o_lat, wuv_ref[...])
        lane_head = lax.broadcasted_iota(I32, (T, H_MLA * DV_MLA), 1) // DV_MLA
        out = jnp.zeros((T, H_MLA * DV_MLA), F32)
        for hh in range(H_MLA):
            out = out + jnp.where(lane_head == hh, o_all[T * hh:T * hh + T, :], 0.0)
        o_ref[...] = out


def _mla_sample(l, cache, pt_flat, n_pages, qc_hm, new_rows, wuv_cat, T):
    B = pt_flat.shape[0] // n_pages
    NP = min(16, n_pages)
    NJ = n_pages // NP
    R = H_MLA * T

    def page_spec(p):
        return pl.BlockSpec((None, None, PAGE, MLA_ROW),
                            lambda b, j, pt: (l, pt[b * n_pages + j * NP + p], 0, 0))
    in_specs = [page_spec(p) for p in range(NP)] + [
        pl.BlockSpec((H_MLA, T, MLA_ROW), lambda b, j, pt: (0, b, 0)),
        pl.BlockSpec((T, MLA_ROW), lambda b, j, pt: (b, 0)),
        pl.BlockSpec((D_KV_LAT, H_MLA * DV_MLA), lambda b, j, pt: (0, 0))]
    grid_spec = pltpu.PrefetchScalarGridSpec(
        num_scalar_prefetch=1, grid=(B, NJ), in_specs=in_specs,
        out_specs=pl.BlockSpec((T, H_MLA * DV_MLA), lambda b, j, pt: (b, 0)),
        scratch_shapes=[pltpu.VMEM((NP * PAGE, MLA_ROW), BF16), pltpu.VMEM((R, 1), F32),
                        pltpu.VMEM((R, 1), F32), pltpu.VMEM((R, MLA_ROW), F32)])
    return pl.pallas_call(
        functools.partial(_mla_sample_kernel, NP, NJ, T),
        grid_spec=grid_spec,
        out_shape=jax.ShapeDtypeStruct((B * T, H_MLA * DV_MLA), F32),
        compiler_params=_cparams(("arbitrary", "arbitrary")),
    )(pt_flat, *([cache] * NP), qc_hm, new_rows, wuv_cat)


def _mix_kernel(oret_ref, odsa_ref, omla_ref, x_ref, g1_ref, sh2_ref, sc2_ref, n2_ref,
                wout_ref, wrh_ref, wrl_ref, br_ref, x1_ref, h2_ref, gate_ref):
    y = (_dot(oret_ref[...].astype(BF16), wout_ref[0:384, :])
         + _dot(odsa_ref[...].astype(BF16), wout_ref[384:640, :])
         + _dot(omla_ref[...].astype(BF16), wout_ref[640:1024, :]))
    x1 = x_ref[...] + g1_ref[...] * y
    x1_ref[...] = x1
    rstd = lax.rsqrt(jnp.mean(x1 * x1, axis=-1, keepdims=True) + EPS)
    h2 = (x1 * rstd) * n2_ref[...] * (1.0 + sc2_ref[...]) + sh2_ref[...]
    h2b = h2.astype(BF16)
    h2_ref[...] = h2b
    logits = _dot(h2b, wrh_ref[...]) + _dot(h2b, wrl_ref[...]) + br_ref[...]
    lane = lax.broadcasted_iota(I32, logits.shape, 1)
    is_g = (lane >= N_EXPERTS) & (lane < N_EXPERTS + N_GROUPS)
    gl = jnp.where(is_g, logits, -jnp.inf)
    gmax = jnp.max(gl, axis=-1, keepdims=True)
    grp = jnp.min(jnp.where(gl == gmax, lane - N_EXPERTS, N_GROUPS), axis=-1, keepdims=True)
    g_prob = 1.0 / jnp.sum(jnp.exp(gl - gmax), axis=-1, keepdims=True)
    in_grp = (lane < N_EXPERTS) & ((lane >> 3) == grp)
    el = jnp.where(in_grp, logits, -jnp.inf)
    m1 = jnp.max(el, axis=-1, keepdims=True)
    i1 = jnp.min(jnp.where(el == m1, lane, LANES), axis=-1, keepdims=True)
    el2 = jnp.where(lane == i1, -jnp.inf, el)
    m2 = jnp.max(el2, axis=-1, keepdims=True)
    i2 = jnp.min(jnp.where(el2 == m2, lane, LANES), axis=-1, keepdims=True)
    e2 = jnp.exp(m2 - m1)
    den = 1.0 + e2
    gate_ref[...] = jnp.where(lane == i1, 1.0 / den, jnp.where(lane == i2, e2 / den, 0.0)) * g_prob


def _mix(o_ret, o_dsa, o_mla, x, gate1, shift2, scale2, n2, lw, tm):
    T = x.shape[0]
    per_tok = gate1.shape[0] != 1

    def row(width):
        return pl.BlockSpec((tm, width), lambda i: (i, 0))

    def const(shape):
        return pl.BlockSpec(shape, lambda i: (0, 0))
    mod_spec = row(D_MODEL) if per_tok else const((1, D_MODEL))
    return pl.pallas_call(
        _mix_kernel, grid=(T // tm,),
        in_specs=[row(384), row(256), row(384), row(D_MODEL), mod_spec, mod_spec, mod_spec,
                  const((1, D_MODEL)), const((D_MODEL, D_MODEL)), const((D_MODEL, LANES)),
                  const((D_MODEL, LANES)), const((1, LANES))],
        out_specs=[row(D_MODEL), row(D_MODEL), row(LANES)],
        out_shape=[jax.ShapeDtypeStruct((T, D_MODEL), F32), jax.ShapeDtypeStruct((T, D_MODEL), BF16),
                   jax.ShapeDtypeStruct((T, LANES), F32)],
        compiler_params=_cparams(("arbitrary",)),
    )(o_ret, o_dsa, o_mla, x, gate1, shift2, scale2, n2, lw["w_out"], lw["w_rt_hi"], lw["w_rt_lo"], lw["b_rt"])


def _moe_kernel(h_ref, gate_ref, x_ref, g2_ref, w13_ref, w2_ref, o_ref, acc_ref):
    e = pl.program_id(1)

    @pl.when(e == 0)
    def _():
        acc_ref[...] = jnp.zeros_like(acc_ref)

    gate = gate_ref[...]
    lane = lax.broadcasted_iota(I32, gate.shape, 1)
    g_col = jnp.sum(jnp.where(lane == e, gate, 0.0), axis=-1, keepdims=True)
    au = _dot(h_ref[...], w13_ref[0])
    a, u = au[:, 0:D_EXPERT], au[:, D_EXPERT:]
    act = (a * _sigmoid(a)) * u * g_col
    acc_ref[...] += _dot(act.astype(BF16), w2_ref[0])

    @pl.when(e == pl.num_programs(1) - 1)
    def _():
        o_ref[...] = x_ref[...] + g2_ref[...] * acc_ref[...]


def _moe(h2, gate, x1, gate2, w13, w2, tm):
    T = h2.shape[0]
    per_tok = gate2.shape[0] != 1
    mod_spec = (pl.BlockSpec((tm, D_MODEL), lambda i, e: (i, 0)) if per_tok
                else pl.BlockSpec((1, D_MODEL), lambda i, e: (0, 0)))
    return pl.pallas_call(
        _moe_kernel, grid=(T // tm, N_EXPERTS),
        in_specs=[pl.BlockSpec((tm, D_MODEL), lambda i, e: (i, 0)),
                  pl.BlockSpec((tm, LANES), lambda i, e: (i, 0)),
                  pl.BlockSpec((tm, D_MODEL), lambda i, e: (i, 0)),
                  mod_spec,
                  pl.BlockSpec((1, D_MODEL, 2 * D_EXPERT), lambda i, e: (e, 0, 0)),
                  pl.BlockSpec((1, D_EXPERT, D_MODEL), lambda i, e: (e, 0, 0))],
        out_specs=pl.BlockSpec((tm, D_MODEL), lambda i, e: (i, 0)),
        out_shape=jax.ShapeDtypeStruct((T, D_MODEL), F32),
        scratch_shapes=[pltpu.VMEM((tm, D_MODEL), F32)],
        compiler_params=_cparams(("arbitrary", "arbitrary")),
    )(h2, gate, x1, gate2, w13, w2)


def _rope_tables(pos, d):
    half = d // 2
    inv = ROPE_THETA ** (-jnp.arange(half, dtype=F32) * 2.0 / d)
    ang = pos.astype(F32)[:, None] * inv[None, :]
    cos, sin = jnp.cos(ang), jnp.sin(ang)
    reps = LANES // d
    return (jnp.tile(jnp.concatenate([cos, cos], axis=1), (1, reps)),
            jnp.tile(jnp.concatenate([-sin, sin], axis=1), (1, reps)))


def _pad_cols(w, width):
    return jnp.pad(w, ((0, 0), (0, width - w.shape[1])))


def _layer_weights(l, w_in, ret_gn_g, dsa_qn_g, dsa_kn_g, idx_kn_g, mla_cqn_g, mla_ckvn_g, mla_krn_g,
                   mla_qn_g, w_uq, w_uk, w_uv, w_out, w_group, b_group, w_router, b_router, w1, w3, w2):
    offs = np.cumsum((0,) + IN_SPLITS)
    cols = [w_in[l][:, offs[k]:offs[k + 1]] for k in range(len(IN_SPLITS))]
    rq, rk, rv, rg, dq, dk, dv, iq, ik, iw, cq, ckv, kr = cols
    w_pack = jnp.concatenate([rq, rk, rv, rg, dq, dk, dv, iq, _pad_cols(ik, 128), cq, ckv,
                              _pad_cols(kr, 128), _pad_cols(iw, 128)], axis=1).astype(BF16)
    uq = w_uq[l].reshape(D_Q_LAT, H_MLA, D_QK_MLA)
    uq_pack = jnp.concatenate([uq[:, :, :D_NOPE].reshape(D_Q_LAT, 384), uq[:, :, D_NOPE:].reshape(D_Q_LAT, 192),
                               jnp.zeros((D_Q_LAT, 64), F32)], axis=1).astype(BF16)
    ukbd = jnp.zeros((384, 768), F32)
    for hh in range(H_MLA):
        ukbd = ukbd.at[64 * hh:64 * hh + 64, 128 * hh:128 * hh + 128].set(w_uk[l, hh])
    g_mq = jnp.concatenate([jnp.tile(mla_qn_g[l, :D_NOPE], H_MLA), jnp.tile(mla_qn_g[l, D_NOPE:], H_MLA),
                            jnp.zeros((64,), F32)])[None, :]
    head_of = np.concatenate([np.repeat(np.arange(H_MLA), D_NOPE), np.repeat(np.arange(H_MLA), D_ROPE),
                              np.full((64,), -1)])
    bq = jnp.asarray((head_of[:, None] == head_of[None, :]) & (head_of[:, None] >= 0), BF16)
    h64 = np.arange(256) // 64
    b64 = jnp.asarray(h64[:, None] == h64[None, :], BF16)
    w_rt = jnp.concatenate([w_router[l], w_group[l], jnp.zeros((D_MODEL, LANES - N_EXPERTS - N_GROUPS), F32)], axis=1)
    rt_hi = w_rt.astype(BF16)
    rt_lo = (w_rt - rt_hi.astype(F32)).astype(BF16)
    b_rt = jnp.concatenate([b_router[l], b_group[l], jnp.zeros((LANES - N_EXPERTS - N_GROUPS,), F32)])[None, :]
    return dict(
        w_pack=w_pack, w_uq=uq_pack, w_ukbd=ukbd.astype(BF16), b64=b64, bq=bq,
        g_dq=jnp.tile(dsa_qn_g[l], H_DSA)[None, :],
        g_kv=_pad_cols(dsa_kn_g[l][None, :], 128), g_ik=_pad_cols(idx_kn_g[l][None, :], 128),
        g_cq=mla_cqn_g[l][None, :], g_ckv=mla_ckvn_g[l][None, :], g_kr=_pad_cols(mla_krn_g[l][None, :], 128),
        g_mq=g_mq, gn=ret_gn_g[l].reshape(1, H_RET * DV_RET),
        wuvt=jnp.swapaxes(w_uv[l], 1, 2).astype(BF16),
        wuv_cat=jnp.transpose(w_uv[l], (1, 0, 2)).reshape(D_KV_LAT, H_MLA * DV_MLA).astype(BF16),
        w_out=w_out[l].astype(BF16), w_rt_hi=rt_hi, w_rt_lo=rt_lo, b_rt=b_rt,
        w13=jnp.concatenate([w1[l], w3[l]], axis=-1).astype(BF16), w2=w2[l].astype(BF16))


def _pick_tm(T, pref):
    tm = min(pref, T)
    while T % tm:
        tm //= 2
    return tm


def kernel(x_prompt, x_sample, cache_dsa, cache_mla, state_ret, page_table, c_prompt, c_sample, norm1_g, norm2_g, w_ada, b_ada, w_in, ret_gn_g, dsa_qn_g, dsa_kn_g, idx_kn_g, mla_cqn_g, mla_ckvn_g, mla_krn_g, mla_qn_g, w_uq, w_uk, w_uv, w_out, w_group, b_group, w_router, b_router, w1, w3, w2):
    depth = w_in.shape[0]
    BP, S, _ = x_prompt.shape
    B, T, _ = x_sample.shape
    assert BP == 1 and S % 256 == 0 and T == 8
    n_pages = page_table.shape[1]
    past = n_pages * PAGE
    topk_p = min(TOPK_MAX, S // 4)
    topk_s = min(TOPK_MAX, (past + T) // 4)
    log_gammas = tuple(float(np.log(np.float32(1.0) - np.float32(2.0) ** np.float32(-5.0 - h))) for h in range(H_RET))
    KC = 256

    n_rows = 1 + B
    r_pad = (-n_rows) % 8
    c_all = jnp.concatenate([c_prompt, c_sample, jnp.zeros((r_pad, D_MODEL), F32)], axis=0)
    mods = _ada(c_all, w_ada, b_ada)

    tabs_p = _rope_tables(jnp.arange(S), 64) + _rope_tables(jnp.arange(S), 32)
    pos_s = jnp.tile(past + jnp.arange(T), B)
    tabs_s = _rope_tables(pos_s, 64) + _rope_tables(pos_s, 32)
    pt_flat = page_table.reshape(-1)

    xp = x_prompt.reshape(S, D_MODEL)
    xs = x_sample.reshape(B * T, D_MODEL)
    tm_p = _pick_tm(S, 256)
    tm_s = _pick_tm(B * T, 256)
    zero_state = jnp.zeros((1, H_RET, DK_RET, DV_RET), F32)
    outs = [[] for _ in range(6)]
    for l in range(depth):
        lw = _layer_weights(l, w_in, ret_gn_g, dsa_qn_g, dsa_kn_g, idx_kn_g, mla_cqn_g, mla_ckvn_g, mla_krn_g,
                            mla_qn_g, w_uq, w_uk, w_uv, w_out, w_group, b_group, w_router, b_router, w1, w3, w2)
        mp = [mods[l, 0:1, k * D_MODEL:(k + 1) * D_MODEL] for k in range(6)]
        ms = [jnp.repeat(mods[l, 1:1 + B, k * D_MODEL:(k + 1) * D_MODEL], T, axis=0) for k in range(6)]
        n1 = norm1_g[l][None, :]
        n2 = norm2_g[l][None, :]

        (rq, rk, rv, rg, dq_hm, iq_hm, _, iwt, drow, kd, ikd, kvt, qc_hm, mrow, kvm, ckvt) = _proj(
            xp, mp[0], mp[1], n1, tabs_p, lw, BF16, tm_p)
        o_ret, st_p = _retention(rq, rk, rv, rg, lw["gn"], zero_state, 128, log_gammas, BF16)
        o_dsa = _dsa_prompt(iq_hm, iwt, dq_hm, ikd, kd, kvt, topk_p, KC)
        o_mla = _mla_prompt(qc_hm, kvm, ckvt, lw["wuvt"], KC)
        x1, h2, gate = _mix(o_ret, o_dsa, o_mla, xp, mp[2], mp[3], mp[4], n2, lw, tm_p)
        xp = _moe(h2, gate, x1, mp[5], lw["w13"], lw["w2"], _pick_tm(S, 1024))
        outs[0].append(drow.reshape(1, S, DSA_ROW))
        outs[2].append(mrow.reshape(1, S, MLA_ROW))
        outs[4].append(st_p)

        (rq, rk, rv, rg, dq_hm, iq_hm, iw, _, drow, _, _, _, qc_hm, mrow, _, _) = _proj(
            xs, ms[0], ms[1], n1, tabs_s, lw, F32, tm_s)
        o_ret, st_s = _retention(rq, rk, rv, rg, lw["gn"], state_ret[l].astype(F32), T, log_gammas, F32)
        o_dsa = _dsa_sample(l, cache_dsa, pt_flat, n_pages, iq_hm, iw, dq_hm, drow, T, topk_s)
        o_mla = _mla_sample(l, cache_mla, pt_flat, n_pages, qc_hm, mrow, lw["wuv_cat"], T)
        x1, h2, gate = _mix(o_ret, o_dsa, o_mla, xs, ms[2], ms[3], ms[4], n2, lw, tm_s)
        xs = _moe(h2, gate, x1, ms[5], lw["w13"], lw["w2"], _pick_tm(B * T, 1024))
        outs[1].append(drow.reshape(B, T, DSA_ROW))
        outs[3].append(mrow.reshape(B, T, MLA_ROW))
        outs[5].append(st_s)

    return (xp.reshape(1, S, D_MODEL), xs.reshape(B, T, D_MODEL),
            jnp.stack(outs[0]), jnp.stack(outs[1]), jnp.stack(outs[2]), jnp.stack(outs[3]),
            jnp.stack(outs[4]), jnp.stack(outs[5]))
```

```python
import functools
import math

import numpy as np
import jax
import jax.numpy as jnp
from jax import lax
from jax.experimental import pallas as pl
from jax.experimental.pallas import tpu as pltpu

F32 = jnp.float32
BF16 = jnp.bfloat16
I32 = jnp.int32

D_MODEL = 1024
PAGE = 128
H_RET, DK_RET, DV_RET = 6, 64, 64
H_DSA, HD_DSA = 4, 64
H_IDX, D_IDX = 8, 64
TOPK_MAX = 256
H_MLA, D_Q_LAT, D_KV_LAT, D_NOPE, D_ROPE, DV_MLA = 6, 256, 128, 64, 32, 64
N_GROUPS, EPG, N_EXPERTS, D_EXPERT = 4, 8, 32, 256
Q_BLOCK = 128
ROPE_THETA = 10000.0
EPS = 1e-6
DSA_ROW = 2 * HD_DSA + D_IDX
MLA_ROW = D_KV_LAT + D_ROPE
D_QK_MLA = D_NOPE + D_ROPE
IN_SPLITS = (384, 384, 384, 384, 256, 64, 64, 512, 64, 8, 256, 128, 32)

C_RQ, C_RK, C_RV, C_RG = 0, 384, 768, 1152
C_DQ, C_KV, C_IQ, C_IK, C_CQ, C_CKV, C_KR, C_IW = 1536, 1792, 1920, 2432, 2560, 2816, 2944, 3072
D_PACK = 3200
MLA_QW = 640

LANES = 128
VMEM_LIMIT = 56 * 1024 * 1024

INT_MIN = -2 ** 31
NEG_BIG = -1e30
LOG2E = math.log2(math.e)
CKVT_ROWS = D_KV_LAT + 16
UNROLL = 2
KEY_NEG_INF = (0xFF800000 ^ 0x7FFFFFFF) - 2 ** 32


def _cparams(sem):
    return pltpu.CompilerParams(dimension_semantics=sem, vmem_limit_bytes=VMEM_LIMIT)


def _dot(a, b):
    return jnp.dot(a, b, preferred_element_type=F32)


def _dot_nt(a, b):
    return lax.dot_general(a, b, (((1,), (1,)), ((), ())), preferred_element_type=F32)


def _split(x):
    hi = x.astype(BF16)
    lo = (x - hi.astype(F32)).astype(BF16)
    return hi, lo


def _dot_f32_lhs(x, b_bf16):
    hi, lo = _split(x)
    return _dot(hi, b_bf16) + _dot(lo, b_bf16)


def _sigmoid(x):
    return 1.0 / (1.0 + jnp.exp(-x))


def _sort_key(x):
    b = lax.bitcast_convert_type(x, I32)
    return b ^ ((b >> 31) & 0x7FFFFFFF)


def _rope(x, cos, sin, half):
    w = x.shape[-1]
    fwd = pltpu.roll(x, w - half, axis=1)
    bwd = pltpu.roll(x, half, axis=1)
    lane = lax.broadcasted_iota(I32, x.shape, 1)
    first = (lane & (2 * half - 1)) < half
    return x * cos + jnp.where(first, fwd, bwd) * sin


def _tile_lanes(t, n):
    return t if n == 1 else jnp.concatenate([t] * n, axis=1)


def _ada_kernel(c_ref, w_ref, b_ref, o_ref):
    c = c_ref[...]
    a = c * _sigmoid(c)
    ah, al = _split(a)
    w = w_ref[0]
    wh, wl = _split(w)
    o_ref[0] = _dot(ah, wh) + _dot(ah, wl) + _dot(al, wh) + b_ref[0]


def _ada(c_all, w_ada, b_ada):
    L = w_ada.shape[0]
    R = c_all.shape[0]
    tn = 1024
    return pl.pallas_call(
        _ada_kernel,
        grid=(L, 6 * D_MODEL // tn),
        in_specs=[pl.BlockSpec((R, D_MODEL), lambda l, j: (0, 0)),
                  pl.BlockSpec((1, D_MODEL, tn), lambda l, j: (l, 0, j)),
                  pl.BlockSpec((1, 1, tn), lambda l, j: (l, 0, j))],
        out_specs=pl.BlockSpec((1, R, tn), lambda l, j: (l, 0, j)),
        out_shape=jax.ShapeDtypeStruct((L, R, 6 * D_MODEL), F32),
        compiler_params=_cparams(("arbitrary", "arbitrary")),
    )(c_all, w_ada, b_ada.reshape(L, 1, 6 * D_MODEL))


def _proj_kernel(x_ref, sh_ref, sc_ref, g1_ref, c64_ref, s64_ref, c32_ref, s32_ref,
                 w_ref, gq_ref, gkv_ref, gik_ref, gcq_ref, gckv_ref, gkr_ref, gmq_ref,
                 wuq_ref, wuk_ref, b64_ref, bq_ref,
                 rq_ref, rk_ref, rv_ref, rg_ref, dq_ref, iq_ref, iw_ref, iwt_ref,
                 drow_ref, kd_ref, ikd_ref, kvt_ref, qc_ref, mrow_ref, kvm_ref, ckvt_ref):
    x = x_ref[...]
    rstd = lax.rsqrt(jnp.mean(x * x, axis=-1, keepdims=True) + EPS)
    h = (x * rstd) * g1_ref[...] * (1.0 + sc_ref[...]) + sh_ref[...]
    hb = h.astype(BF16)
    c64, s64, c32, s32 = c64_ref[...], s64_ref[...], c32_ref[...], s32_ref[...]

    def zcols(c0, width):
        return _dot(hb, w_ref[:, c0:c0 + width])

    rq_ref[...] = _rope(zcols(C_RQ, 384), _tile_lanes(c64, 3), _tile_lanes(s64, 3), 32)
    rk_ref[...] = _rope(zcols(C_RK, 384), _tile_lanes(c64, 3), _tile_lanes(s64, 3), 32) * (DK_RET ** -0.5)
    rv_ref[...] = zcols(C_RV, 384)
    rg_ref[...] = zcols(C_RG, 384)

    dq = zcols(C_DQ, 256)
    ss = _dot_f32_lhs(dq * dq, b64_ref[...])
    dq = dq * lax.rsqrt(ss * (1.0 / HD_DSA) + EPS) * gq_ref[...]
    dq = _rope(dq, _tile_lanes(c64, 2), _tile_lanes(s64, 2), 32) * (HD_DSA ** -0.5 * LOG2E)
    for hh in range(H_DSA):
        dq_ref[hh] = dq[:, 64 * hh:64 * hh + 64].astype(dq_ref.dtype)

    kv = zcols(C_KV, 128)
    lane = lax.broadcasted_iota(I32, kv.shape, 1)
    is_k = lane < HD_DSA
    ssk = jnp.sum(jnp.where(is_k, kv * kv, 0.0), axis=-1, keepdims=True)
    kn = kv * lax.rsqrt(ssk * (1.0 / HD_DSA) + EPS) * gkv_ref[...]
    kv = jnp.where(is_k, _rope(kn, c64, s64, 32), kv)
    drow_ref[:, 0:128] = kv
    kvb = kv.astype(BF16)
    kd_ref[...] = kvb[:, 0:64]
    kvt_ref[0] = jnp.where(is_k, 1.0, kv).T.astype(BF16)

    iq = _rope(zcols(C_IQ, 512), _tile_lanes(c64, 4), _tile_lanes(s64, 4), 32) * (D_IDX ** -0.5)
    for hh in range(H_IDX):
        iq_ref[hh] = iq[:, 64 * hh:64 * hh + 64].astype(iq_ref.dtype)
    ik = zcols(C_IK, 128)
    ssi = jnp.sum(ik * ik, axis=-1, keepdims=True)
    ik = _rope(ik * lax.rsqrt(ssi * (1.0 / D_IDX) + EPS) * gik_ref[...], c64, s64, 32)
    drow_ref[:, 128:192] = ik[:, 0:64]
    ikd_ref[...] = ik[:, 0:64].astype(BF16)
    iw = zcols(C_IW, 128) * (H_IDX ** -0.5)
    iw_ref[...] = iw
    iwt_ref[...] = iw.T[0:H_IDX, :]

    cq = zcols(C_CQ, 256)
    cq = cq * lax.rsqrt(jnp.mean(cq * cq, axis=-1, keepdims=True) + EPS) * gcq_ref[...]
    q = _dot(cq.astype(BF16), wuq_ref[...])
    ssq = _dot_f32_lhs(q * q, bq_ref[...])
    q = q * lax.rsqrt(ssq * (1.0 / D_QK_MLA) + EPS) * gmq_ref[...]
    scale = D_QK_MLA ** -0.5 * LOG2E
    q_lat = _dot(q[:, 0:384].astype(BF16), wuk_ref[...]) * scale
    q_rope = _rope(q[:, 384:640], _tile_lanes(c32, 2), _tile_lanes(s32, 2), 16) * scale
    for hh in range(H_MLA):
        qc_ref[hh, :, 0:128] = q_lat[:, 128 * hh:128 * hh + 128].astype(qc_ref.dtype)
        qc_ref[hh, :, 128:160] = q_rope[:, 32 * hh:32 * hh + 32].astype(qc_ref.dtype)
    ckv = zcols(C_CKV, 128)
    ckv = ckv * lax.rsqrt(jnp.mean(ckv * ckv, axis=-1, keepdims=True) + EPS) * gckv_ref[...]
    kr = zcols(C_KR, 128)
    ssr = jnp.sum(kr * kr, axis=-1, keepdims=True)
    kr = _rope(kr * lax.rsqrt(ssr * (1.0 / D_ROPE) + EPS) * gkr_ref[...], c32, s32, 16)
    mrow_ref[:, 0:128] = ckv
    mrow_ref[:, 128:160] = kr[:, 0:32]
    kvm_ref[:, 0:128] = ckv.astype(BF16)
    kvm_ref[:, 128:160] = kr[:, 0:32].astype(BF16)
    ckvt_ref[0, 0:D_KV_LAT, :] = ckv.T.astype(BF16)
    ckvt_ref[0, D_KV_LAT:CKVT_ROWS, :] = jnp.ones((CKVT_ROWS - D_KV_LAT, ckv.shape[0]), BF16)


def _proj(x, shift, scale, g1, tabs, lw, qdtype, tm):
    T = x.shape[0]
    nb = T // tm
    per_tok = shift.shape[0] != 1

    def row(width):
        return pl.BlockSpec((tm, width), lambda i: (i, 0))

    def const(shape):
        nd = len(shape)
        return pl.BlockSpec(shape, lambda i: (0,) * nd)

    mod_spec = row(D_MODEL) if per_tok else const((1, D_MODEL))
    in_specs = [row(D_MODEL), mod_spec, mod_spec, const((1, D_MODEL)),
                row(LANES), row(LANES), row(LANES), row(LANES),
                const((D_MODEL, D_PACK)),
                const((1, 256)), const((1, 128)), const((1, 128)), const((1, 256)),
                const((1, 128)), const((1, 128)), const((1, MLA_QW)),
                const((D_Q_LAT, MLA_QW)), const((384, 768)), const((256, 256)), const((MLA_QW, MLA_QW))]
    out_shape = [
        jax.ShapeDtypeStruct((T, 384), F32), jax.ShapeDtypeStruct((T, 384), F32),
        jax.ShapeDtypeStruct((T, 384), F32), jax.ShapeDtypeStruct((T, 384), F32),
        jax.ShapeDtypeStruct((H_DSA, T, 64), qdtype), jax.ShapeDtypeStruct((H_IDX, T, 64), qdtype),
        jax.ShapeDtypeStruct((T, LANES), F32), jax.ShapeDtypeStruct((H_IDX, T), F32),
        jax.ShapeDtypeStruct((T, DSA_ROW), F32), jax.ShapeDtypeStruct((T, 64), BF16),
        jax.ShapeDtypeStruct((T, 64), BF16), jax.ShapeDtypeStruct((nb, 128, tm), BF16),
        jax.ShapeDtypeStruct((H_MLA, T, 160), qdtype), jax.ShapeDtypeStruct((T, MLA_ROW), F32),
        jax.ShapeDtypeStruct((T, MLA_ROW), BF16), jax.ShapeDtypeStruct((nb, CKVT_ROWS, tm), BF16)]
    out_specs = [
        row(384), row(384), row(384), row(384),
        pl.BlockSpec((H_DSA, tm, 64), lambda i: (0, i, 0)), pl.BlockSpec((H_IDX, tm, 64), lambda i: (0, i, 0)),
        row(LANES), pl.BlockSpec((H_IDX, tm), lambda i: (0, i)),
        row(DSA_ROW), row(64), row(64), pl.BlockSpec((1, 128, tm), lambda i: (i, 0, 0)),
        pl.BlockSpec((H_MLA, tm, 160), lambda i: (0, i, 0)), row(MLA_ROW), row(MLA_ROW),
        pl.BlockSpec((1, CKVT_ROWS, tm), lambda i: (i, 0, 0))]
    return pl.pallas_call(
        _proj_kernel, grid=(nb,), in_specs=in_specs, out_specs=out_specs, out_shape=out_shape,
        compiler_params=_cparams(("arbitrary",)),
    )(x, shift, scale, g1, *tabs, lw["w_pack"], lw["g_dq"], lw["g_kv"], lw["g_ik"], lw["g_cq"],
      lw["g_ckv"], lw["g_kr"], lw["g_mq"], lw["w_uq"], lw["w_ukbd"], lw["b64"], lw["bq"])


def _ret_kernel(log_gammas, C, q_ref, k_ref, v_ref, g_ref, gn_ref, s0_ref, o_ref, st_ref):
    first = pl.program_id(1) == 0

    @pl.when(first)
    def _():
        st_ref[...] = s0_ref[...]

    ri = lax.broadcasted_iota(I32, (C, C), 0)
    ci = lax.broadcasted_iota(I32, (C, C), 1)
    rel = (ri - ci).astype(F32)
    pos = lax.broadcasted_iota(I32, (C, 1), 0).astype(F32)
    q_all, k_all, v_all, g_all, gn = q_ref[...], k_ref[...], v_ref[...], g_ref[...], gn_ref[...]
    for hh in range(H_RET):
        lg = log_gammas[hh]
        sl = slice(64 * hh, 64 * hh + 64)
        q, k, v = q_all[:, sl], k_all[:, sl], v_all[:, sl]
        qb, vb = q.astype(BF16), v.astype(BF16)
        decay = jnp.where(rel >= 0, jnp.exp(lg * jnp.maximum(rel, 0.0)), 0.0)
        attn = _dot_nt(qb, k.astype(BF16)) * decay
        state = st_ref[0, hh]
        o = _dot(attn.astype(BF16), vb) + _dot(qb, state.astype(BF16)) * jnp.exp(lg * (pos + 1.0))
        k_dec = k * jnp.exp(lg * (C - 1.0 - pos))
        st_ref[0, hh] = state * math.exp(lg * C) + _dot(k_dec.T.astype(BF16), vb)
        mu = jnp.mean(o, axis=-1, keepdims=True)
        d = o - mu
        var = jnp.mean(d * d, axis=-1, keepdims=True)
        gate = g_all[:, sl]
        o_ref[:, sl] = (d * lax.rsqrt(var + EPS) * gn[:, sl] * (gate * _sigmoid(gate))).astype(o_ref.dtype)


def _retention(rq, rk, rv, rg, gn, state0, C, log_gammas, out_dtype):
    NB = state0.shape[0]
    T = rq.shape[0]
    nc = T // (NB * C)
    row = pl.BlockSpec((C, 384), lambda b, c: (b * nc + c, 0))
    st = pl.BlockSpec((1, H_RET, DK_RET, DV_RET), lambda b, c: (b, 0, 0, 0))
    return pl.pallas_call(
        functools.partial(_ret_kernel, log_gammas, C),
        grid=(NB, nc),
        in_specs=[row, row, row, row, pl.BlockSpec((1, 384), lambda b, c: (0, 0)), st],
        out_specs=[row, st],
        out_shape=[jax.ShapeDtypeStruct((T, 384), out_dtype),
                   jax.ShapeDtypeStruct((NB, H_RET, DK_RET, DV_RET), F32)],
        compiler_params=_cparams(("arbitrary", "arbitrary")),
    )(rq, rk, rv, rg, gn, state0)


def _count_ge(keys_ref, nch, kc, cand):
    def body(c, acc):
        blk = keys_ref[pl.ds(pl.multiple_of(c * kc, kc), kc), :]
        ind = jnp.where(blk >= cand, 1, 0).astype(I32)
        return acc + jnp.sum(ind.reshape(kc // 8, 8, LANES), axis=0)
    acc = lax.fori_loop(0, nch, body, jnp.zeros((8, LANES), I32))
    return jnp.sum(acc, axis=0, keepdims=True)


def _kth_largest_key(count_fn, topk, shape):
    def cond(st):
        it, _, cnt_acc = st
        return (it < 32) & (jnp.max(cnt_acc) > topk)

    def body(st):
        it, t_u, cnt_acc = st
        for sub in range(2):
            cand_u = t_u | (jnp.int32(1) << (31 - it - sub))
            cnt = count_fn(cand_u ^ INT_MIN)
            take = cnt >= topk
            t_u, cnt_acc = jnp.where(take, cand_u, t_u), jnp.where(take, cnt, cnt_acc)
        return it + 2, t_u, cnt_acc
    init = (jnp.int32(0), jnp.zeros(shape, I32), jnp.full(shape, 2 ** 30, I32))
    _, t_u, _ = lax.while_loop(cond, body, init)
    return t_u ^ INT_MIN


def _dsa_prompt_kernel(topk, kc, iq_ref, iwt_ref, dq_ref, ikd_ref, kd_ref, kvt_ref, o_ref, keys_ref):
    i = pl.program_id(0)
    sup = kc * UNROLL
    last = (i * Q_BLOCK) // sup
    iqs = iq_ref[...].reshape(H_IDX * Q_BLOCK, D_IDX)
    iwt = iwt_ref[...]

    def idx_scores(start):
        s = _dot_nt(ikd_ref[pl.ds(start, kc), :], iqs)
        score = jnp.maximum(s[:, 0:LANES], 0.0) * iwt[0:1, :]
        for hh in range(1, H_IDX):
            score = score + jnp.maximum(s[:, LANES * hh:LANES * hh + LANES], 0.0) * iwt[hh:hh + 1, :]
        return score

    def idx_body(c, carry):
        for u in range(UNROLL):
            start = pl.multiple_of((c * UNROLL + u) * kc, kc)
            keys_ref[pl.ds(start, kc), :] = _sort_key(idx_scores(start))
        return carry
    lax.fori_loop(0, last, idx_body, 0)
    q_pos = i * Q_BLOCK + lax.broadcasted_iota(I32, (kc, LANES), 1)
    for u in range(UNROLL):
        start = pl.multiple_of((last * UNROLL + u) * kc, kc)
        key_pos = start + lax.broadcasted_iota(I32, (kc, LANES), 0)
        keys_ref[pl.ds(start, kc), :] = _sort_key(jnp.where(key_pos <= q_pos, idx_scores(start), -jnp.inf))

    thr = _kth_largest_key(lambda cand: _count_ge(keys_ref, (last + 1) * UNROLL, kc, cand), topk, (1, LANES))
    thr = jnp.maximum(thr, KEY_NEG_INF + 1)

    dqs = dq_ref[...].reshape(H_DSA * Q_BLOCK, HD_DSA)
    W = H_DSA * LANES

    def att_step(cs, carry):
        m, acc = carry
        start = pl.multiple_of(cs * kc, kc)
        s = _dot_nt(kd_ref[pl.ds(start, kc), :], dqs)
        sel = keys_ref[pl.ds(start, kc), :] >= thr
        s = jnp.concatenate([jnp.where(sel, s[:, LANES * hh:LANES * hh + LANES], NEG_BIG)
                             for hh in range(H_DSA)], axis=1)
        m_new = jnp.maximum(m, jnp.max(s, axis=0, keepdims=True))
        p = jnp.exp2(s - m_new)
        acc = jnp.exp2(m - m_new) * acc + _dot(kvt_ref[cs], p.astype(BF16))
        return m_new, acc

    def att_body(c, carry):
        for u in range(UNROLL):
            carry = att_step(c * UNROLL + u, carry)
        return carry
    m0 = jnp.full((1, W), NEG_BIG, F32)
    m, acc = lax.fori_loop(0, last + 1, att_body, (m0, jnp.zeros((LANES, W), F32)))
    o_t = acc[HD_DSA:, :] / acc[0:1, :]
    o_hq = jnp.concatenate([o_t[:, LANES * hh:LANES * hh + LANES] for hh in range(H_DSA)], axis=0)
    o_ref[...] = o_hq.T.astype(o_ref.dtype)


def _dsa_prompt(iq_hm, iwt, dq_hm, ikd, kd, kvt, topk, kc):
    T = ikd.shape[0]
    nq = T // Q_BLOCK
    nkb = kvt.shape[0]
    return pl.pallas_call(
        functools.partial(_dsa_prompt_kernel, topk, kc),
        grid=(nq,),
        in_specs=[pl.BlockSpec((H_IDX, Q_BLOCK, D_IDX), lambda i: (0, i, 0)),
                  pl.BlockSpec((H_IDX, Q_BLOCK), lambda i: (0, i)),
                  pl.BlockSpec((H_DSA, Q_BLOCK, HD_DSA), lambda i: (0, i, 0)),
                  pl.BlockSpec((T, D_IDX), lambda i: (0, 0)),
                  pl.BlockSpec((T, HD_DSA), lambda i: (0, 0)),
                  pl.BlockSpec((nkb, 128, kc), lambda i: (0, 0, 0))],
        out_specs=pl.BlockSpec((Q_BLOCK, H_DSA * HD_DSA), lambda i: (i, 0)),
        out_shape=jax.ShapeDtypeStruct((T, H_DSA * HD_DSA), BF16),
        scratch_shapes=[pltpu.VMEM((nkb * kc, LANES), I32)],
        compiler_params=_cparams(("arbitrary",)),
    )(iq_hm, iwt, dq_hm, ikd, kd, kvt)


def _mla_prompt_kernel(kc, qc_ref, kvm_ref, ckvt_ref, wuvt_ref, o_ref):
    i = pl.program_id(0)
    last = (i * Q_BLOCK) // (kc * UNROLL)
    W = H_MLA * LANES
    qs = qc_ref[...].reshape(H_MLA * Q_BLOCK, MLA_ROW)

    def step(c, carry, causal):
        m, acc = carry
        start = pl.multiple_of(c * kc, kc)
        s = _dot_nt(kvm_ref[pl.ds(start, kc), :], qs)
        if causal:
            q_pos = i * Q_BLOCK + (lax.broadcasted_iota(I32, (kc, W), 1) & (LANES - 1))
            key_pos = start + lax.broadcasted_iota(I32, (kc, W), 0)
            s = jnp.where(key_pos <= q_pos, s, NEG_BIG)
        m_new = jnp.maximum(m, jnp.max(s, axis=0, keepdims=True))
        p = jnp.exp2(s - m_new)
        acc = jnp.exp2(m - m_new) * acc + _dot(ckvt_ref[c], p.astype(BF16))
        return m_new, acc
    carry = (jnp.full((1, W), NEG_BIG, F32), jnp.zeros((CKVT_ROWS, W), F32))

    def body(c, cr):
        for u in range(UNROLL):
            cr = step(c * UNROLL + u, cr, False)
        return cr
    carry = lax.fori_loop(0, last, body, carry)
    for u in range(UNROLL):
        carry = step(last * UNROLL + u, carry, True)
    m, acc = carry
    o_lat = (acc[0:D_KV_LAT, :] / acc[D_KV_LAT:D_KV_LAT + 1, :]).astype(BF16)
    outs = [_dot(wuvt_ref[hh], o_lat[:, LANES * hh:LANES * hh + LANES]) for hh in range(H_MLA)]
    o_ref[...] = jnp.concatenate(outs, axis=0).T.astype(o_ref.dtype)


def _mla_prompt(qc_hm, kvm, ckvt, wuvt, kc):
    T = kvm.shape[0]
    nq = T // Q_BLOCK
    nkb = ckvt.shape[0]
    return pl.pallas_call(
        functools.partial(_mla_prompt_kernel, kc),
        grid=(nq,),
        in_specs=[pl.BlockSpec((H_MLA, Q_BLOCK, MLA_ROW), lambda i: (0, i, 0)),
                  pl.BlockSpec((T, MLA_ROW), lambda i: (0, 0)),
                  pl.BlockSpec((nkb, CKVT_ROWS, kc), lambda i: (0, 0, 0)),
                  pl.BlockSpec((H_MLA, DV_MLA, D_KV_LAT), lambda i: (0, 0, 0))],
        out_specs=pl.BlockSpec((Q_BLOCK, H_MLA * DV_MLA), lambda i: (i, 0)),
        out_shape=jax.ShapeDtypeStruct((T, H_MLA * DV_MLA), BF16),
        compiler_params=_cparams(("arbitrary",)),
    )(qc_hm, kvm, ckvt, wuvt)


def _dsa_sample_kernel(topk, NP, NJ, T, *refs):
    pages = refs[1:1 + NP]
    iq_ref, iw_ref, dq_ref, new_ref, o_ref, rowst_ref, newp_ref, keys_ref, satt_ref = refs[1 + NP:]
    j = pl.program_id(1)
    CH = NP * PAGE
    iqs = iq_ref[...].reshape(H_IDX * T, D_IDX).astype(BF16)
    dqs = dq_ref[...].reshape(H_DSA * T, HD_DSA).astype(BF16)
    iw = iw_ref[...]

    def head_sum(s):
        score = jnp.maximum(s[0:T, :], 0.0) * iw[:, 0:1]
        for hh in range(1, H_IDX):
            score = score + jnp.maximum(s[T * hh:T * hh + T, :], 0.0) * iw[:, hh:hh + 1]
        return score

    def per_head(sel, s):
        return jnp.concatenate([jnp.where(sel, s[T * hh:T * hh + T, :], NEG_BIG) for hh in range(H_DSA)], axis=0)

    chunk = jnp.concatenate([pages[p][...] for p in range(NP)], axis=1).astype(BF16)
    rowst_ref[j] = chunk
    keys_ref[j] = _sort_key(head_sum(_dot(iqs, chunk[2 * HD_DSA:DSA_ROW, :])))
    satt_ref[j] = _dot(dqs, chunk[0:HD_DSA, :])

    @pl.when(j == NJ - 1)
    def _():
        newp_ref[...] = jnp.zeros((PAGE, DSA_ROW), BF16)
        newp_ref[0:T, :] = new_ref[...].astype(BF16)
        newp = newp_ref[...]
        qi = lax.broadcasted_iota(I32, (T, PAGE), 0)
        kj = lax.broadcasted_iota(I32, (T, PAGE), 1)
        idx_new = head_sum(_dot_nt(iqs, newp[:, 2 * HD_DSA:DSA_ROW]))
        keys_new = _sort_key(jnp.where(kj <= qi, idx_new, -jnp.inf))
        s_new = _dot_nt(dqs, newp[:, 0:HD_DSA])

        def count_fn(cand):
            accs = [jnp.where(keys_new >= cand, 1, 0).astype(I32)] + [jnp.zeros((T, LANES), I32)] * 3
            for c in range(NJ):
                ind = jnp.where(keys_ref[c] >= cand, 1, 0).astype(I32)
                for g in range(CH // LANES):
                    accs[g % 4] = accs[g % 4] + ind[:, LANES * g:LANES * g + LANES]
            return jnp.sum((accs[0] + accs[1]) + (accs[2] + accs[3]), axis=1, keepdims=True)
        thr = jnp.maximum(_kth_largest_key(count_fn, topk, (T, 1)), KEY_NEG_INF + 1)

        sn = per_head(keys_new >= thr, s_new)
        m = jnp.max(sn, axis=1, keepdims=True)
        for c in range(NJ):
            m = jnp.maximum(m, jnp.max(per_head(keys_ref[c] >= thr, satt_ref[c]), axis=1, keepdims=True))
        pn = jnp.exp2(sn - m)
        l = jnp.sum(pn, axis=1, keepdims=True)
        acc = _dot(pn.astype(BF16), newp[:, HD_DSA:2 * HD_DSA])
        for c in range(NJ):
            p = jnp.exp2(per_head(keys_ref[c] >= thr, satt_ref[c]) - m)
            l = l + jnp.sum(p, axis=1, keepdims=True)
            acc = acc + _dot_nt(p.astype(BF16), rowst_ref[c, HD_DSA:2 * HD_DSA, :])
        o = acc / l
        for hh in range(H_DSA):
            o_ref[:, 64 * hh:64 * hh + 64] = o[T * hh:T * hh + T, :]


def _dsa_sample(l, cache, pt_flat, n_pages, iq_hm, iw, dq_hm, new_rows, T, topk):
    B = pt_flat.shape[0] // n_pages
    NP = min(16, n_pages)
    NJ = n_pages // NP
    CH = NP * PAGE

    def page_spec(p):
        return pl.BlockSpec((None, None, DSA_ROW, PAGE),
                            lambda b, j, pt: (l, pt[b * n_pages + j * NP + p], 0, 0))
    in_specs = [page_spec(p) for p in range(NP)] + [
        pl.BlockSpec((H_IDX, T, D_IDX), lambda b, j, pt: (0, b, 0)),
        pl.BlockSpec((T, LANES), lambda b, j, pt: (b, 0)),
        pl.BlockSpec((H_DSA, T, HD_DSA), lambda b, j, pt: (0, b, 0)),
        pl.BlockSpec((T, DSA_ROW), lambda b, j, pt: (b, 0))]
    grid_spec = pltpu.PrefetchScalarGridSpec(
        num_scalar_prefetch=1, grid=(B, NJ), in_specs=in_specs,
        out_specs=pl.BlockSpec((T, H_DSA * HD_DSA), lambda b, j, pt: (b, 0)),
        scratch_shapes=[pltpu.VMEM((NJ, DSA_ROW, CH), BF16), pltpu.VMEM((PAGE, DSA_ROW), BF16),
                        pltpu.VMEM((NJ, T, CH), I32), pltpu.VMEM((NJ, H_DSA * T, CH), F32)])
    return pl.pallas_call(
        functools.partial(_dsa_sample_kernel, topk, NP, NJ, T),
        grid_spec=grid_spec,
        out_shape=jax.ShapeDtypeStruct((B * T, H_DSA * HD_DSA), F32),
        compiler_params=_cparams(("arbitrary", "arbitrary")),
    )(pt_flat, *([cache] * NP), iq_hm, iw, dq_hm, new_rows)


def _mla_sample_kernel(NP, NJ, T, *refs):
    pages = refs[1:1 + NP]
    qc_ref, new_ref, wuv_ref, o_ref, newp_ref, m_ref, l_ref, acc_ref = refs[1 + NP:]
    j = pl.program_id(1)
    R = H_MLA * T
    qs = qc_ref[...].reshape(R, MLA_ROW).astype(BF16)

    @pl.when(j == 0)
    def _():
        m_ref[...] = jnp.full((R, 1), NEG_BIG, F32)
        l_ref[...] = jnp.zeros((R, 1), F32)
        acc_ref[...] = jnp.zeros((R, D_KV_LAT), F32)

    def update(s, pv):
        m_old = m_ref[...]
        m_new = jnp.maximum(m_old, jnp.max(s, axis=1, keepdims=True))
        alpha = jnp.exp2(m_old - m_new)
        p = jnp.exp2(s - m_new)
        l_ref[...] = alpha * l_ref[...] + jnp.sum(p, axis=1, keepdims=True)
        acc_ref[...] = alpha * acc_ref[...] + pv(p.astype(BF16))
        m_ref[...] = m_new

    chunk = jnp.concatenate([pages[p][...] for p in range(NP)], axis=1).astype(BF16)
    update(_dot(qs, chunk), lambda p: _dot_nt(p, chunk[0:D_KV_LAT, :]))

    @pl.when(j == NJ - 1)
    def _():
        newp_ref[...] = jnp.zeros((PAGE, MLA_ROW), BF16)
        newp_ref[0:T, :] = new_ref[...].astype(BF16)
        newp = newp_ref[...]
        qi = lax.broadcasted_iota(I32, (R, PAGE), 0) & (T - 1)
        kj = lax.broadcasted_iota(I32, (R, PAGE), 1)
        update(jnp.where(kj <= qi, _dot_nt(qs, newp), NEG_BIG), lambda p: _dot(p, newp[:, 0:D_KV_LAT]))
        o_lat = (acc_ref[...] / l_ref[...]).astype(BF16)
        o_all = _dot(o_lat, wuv_ref[...])
        lane_head = lax.broadcasted_iota(I32, (T, H_MLA * DV_MLA), 1) // DV_MLA
        out = jnp.zeros((T, H_MLA * DV_MLA), F32)
        for hh in range(H_MLA):
            out = out + jnp.where(lane_head == hh, o_all[T * hh:T * hh + T, :], 0.0)
        o_ref[...] = out


def _mla_sample(l, cache, pt_flat, n_pages, qc_hm, new_rows, wuv_cat, T):
    B = pt_flat.shape[0] // n_pages
    NP = min(16, n_pages)
    NJ = n_pages // NP
    R = H_MLA * T

    def page_spec(p):
        return pl.BlockSpec((None, None, MLA_ROW, PAGE),
                            lambda b, j, pt: (l, pt[b * n_pages + j * NP + p], 0, 0))
    in_specs = [page_spec(p) for p in range(NP)] + [
        pl.BlockSpec((H_MLA, T, MLA_ROW), lambda b, j, pt: (0, b, 0)),
        pl.BlockSpec((T, MLA_ROW), lambda b, j, pt: (b, 0)),
        pl.BlockSpec((D_KV_LAT, H_MLA * DV_MLA), lambda b, j, pt: (0, 0))]
    grid_spec = pltpu.PrefetchScalarGridSpec(
        num_scalar_prefetch=1, grid=(B, NJ), in_specs=in_specs,
        out_specs=pl.BlockSpec((T, H_MLA * DV_MLA), lambda b, j, pt: (b, 0)),
        scratch_shapes=[pltpu.VMEM((PAGE, MLA_ROW), BF16), pltpu.VMEM((R, 1), F32),
                        pltpu.VMEM((R, 1), F32), pltpu.VMEM((R, D_KV_LAT), F32)])
    return pl.pallas_call(
        functools.partial(_mla_sample_kernel, NP, NJ, T),
        grid_spec=grid_spec,
        out_shape=jax.ShapeDtypeStruct((B * T, H_MLA * DV_MLA), F32),
        compiler_params=_cparams(("arbitrary", "arbitrary")),
    )(pt_flat, *([cache] * NP), qc_hm, new_rows, wuv_cat)


def _mix_kernel(oret_ref, odsa_ref, omla_ref, x_ref, g1_ref, sh2_ref, sc2_ref, n2_ref,
                wout_ref, wrh_ref, wrl_ref, br_ref, x1_ref, h2_ref, gate_ref):
    y = (_dot(oret_ref[...].astype(BF16), wout_ref[0:384, :])
         + _dot(odsa_ref[...].astype(BF16), wout_ref[384:640, :])
         + _dot(omla_ref[...].astype(BF16), wout_ref[640:1024, :]))
    x1 = x_ref[...] + g1_ref[...] * y
    x1_ref[...] = x1
    rstd = lax.rsqrt(jnp.mean(x1 * x1, axis=-1, keepdims=True) + EPS)
    h2 = (x1 * rstd) * n2_ref[...] * (1.0 + sc2_ref[...]) + sh2_ref[...]
    h2b = h2.astype(BF16)
    h2_ref[...] = h2b
    logits = _dot(h2b, wrh_ref[...]) + _dot(h2b, wrl_ref[...]) + br_ref[...]
    lane = lax.broadcasted_iota(I32, logits.shape, 1)
    is_g = (lane >= N_EXPERTS) & (lane < N_EXPERTS + N_GROUPS)
    gl = jnp.where(is_g, logits, -jnp.inf)
    gmax = jnp.max(gl, axis=-1, keepdims=True)
    grp = jnp.min(jnp.where(gl == gmax, lane - N_EXPERTS, N_GROUPS), axis=-1, keepdims=True)
    g_prob = 1.0 / jnp.sum(jnp.exp(gl - gmax), axis=-1, keepdims=True)
    in_grp = (lane < N_EXPERTS) & ((lane >> 3) == grp)
    el = jnp.where(in_grp, logits, -jnp.inf)
    m1 = jnp.max(el, axis=-1, keepdims=True)
    i1 = jnp.min(jnp.where(el == m1, lane, LANES), axis=-1, keepdims=True)
    el2 = jnp.where(lane == i1, -jnp.inf, el)
    m2 = jnp.max(el2, axis=-1, keepdims=True)
    i2 = jnp.min(jnp.where(el2 == m2, lane, LANES), axis=-1, keepdims=True)
    e2 = jnp.exp(m2 - m1)
    den = 1.0 + e2
    gate_ref[...] = jnp.where(lane == i1, 1.0 / den, jnp.where(lane == i2, e2 / den, 0.0)) * g_prob


def _mix(o_ret, o_dsa, o_mla, x, gate1, shift2, scale2, n2, lw, tm):
    T = x.shape[0]
    per_tok = gate1.shape[0] != 1

    def row(width):
        return pl.BlockSpec((tm, width), lambda i: (i, 0))

    def const(shape):
        return pl.BlockSpec(shape, lambda i: (0, 0))
    mod_spec = row(D_MODEL) if per_tok else const((1, D_MODEL))
    return pl.pallas_call(
        _mix_kernel, grid=(T // tm,),
        in_specs=[row(384), row(256), row(384), row(D_MODEL), mod_spec, mod_spec, mod_spec,
                  const((1, D_MODEL)), const((D_MODEL, D_MODEL)), const((D_MODEL, LANES)),
                  const((D_MODEL, LANES)), const((1, LANES))],
        out_specs=[row(D_MODEL), row(D_MODEL), row(LANES)],
        out_shape=[jax.ShapeDtypeStruct((T, D_MODEL), F32), jax.ShapeDtypeStruct((T, D_MODEL), BF16),
                   jax.ShapeDtypeStruct((T, LANES), F32)],
        compiler_params=_cparams(("arbitrary",)),
    )(o_ret, o_dsa, o_mla, x, gate1, shift2, scale2, n2, lw["w_out"], lw["w_rt_hi"], lw["w_rt_lo"], lw["b_rt"])


def _moe_kernel(h_ref, gate_ref, x_ref, g2_ref, w13_ref, w2_ref, o_ref, acc_ref):
    e = pl.program_id(1)

    @pl.when(e == 0)
    def _():
        acc_ref[...] = jnp.zeros_like(acc_ref)

    gate = gate_ref[...]
    lane = lax.broadcasted_iota(I32, gate.shape, 1)
    g_col = jnp.sum(jnp.where(lane == e, gate, 0.0), axis=-1, keepdims=True)
    au = _dot(h_ref[...], w13_ref[0])
    a, u = au[:, 0:D_EXPERT], au[:, D_EXPERT:]
    act = (a * _sigmoid(a)) * u * g_col
    acc_ref[...] += _dot(act.astype(BF16), w2_ref[0])

    @pl.when(e == pl.num_programs(1) - 1)
    def _():
        o_ref[...] = x_ref[...] + g2_ref[...] * acc_ref[...]


def _moe(h2, gate, x1, gate2, w13, w2, tm):
    T = h2.shape[0]
    per_tok = gate2.shape[0] != 1
    mod_spec = (pl.BlockSpec((tm, D_MODEL), lambda i, e: (i, 0)) if per_tok
                else pl.BlockSpec((1, D_MODEL), lambda i, e: (0, 0)))
    return pl.pallas_call(
        _moe_kernel, grid=(T // tm, N_EXPERTS),
        in_specs=[pl.BlockSpec((tm, D_MODEL), lambda i, e: (i, 0)),
                  pl.BlockSpec((tm, LANES), lambda i, e: (i, 0)),
                  pl.BlockSpec((tm, D_MODEL), lambda i, e: (i, 0)),
                  mod_spec,
                  pl.BlockSpec((1, D_MODEL, 2 * D_EXPERT), lambda i, e: (e, 0, 0)),
                  pl.BlockSpec((1, D_EXPERT, D_MODEL), lambda i, e: (e, 0, 0))],
        out_specs=pl.BlockSpec((tm, D_MODEL), lambda i, e: (i, 0)),
        out_shape=jax.ShapeDtypeStruct((T, D_MODEL), F32),
        scratch_shapes=[pltpu.VMEM((tm, D_MODEL), F32)],
        compiler_params=_cparams(("arbitrary", "arbitrary")),
    )(h2, gate, x1, gate2, w13, w2)


def _rope_tables(pos, d):
    half = d // 2
    inv = ROPE_THETA ** (-jnp.arange(half, dtype=F32) * 2.0 / d)
    ang = pos.astype(F32)[:, None] * inv[None, :]
    cos, sin = jnp.cos(ang), jnp.sin(ang)
    reps = LANES // d
    return (jnp.tile(jnp.concatenate([cos, cos], axis=1), (1, reps)),
            jnp.tile(jnp.concatenate([-sin, sin], axis=1), (1, reps)))


def _pad_cols(w, width):
    return jnp.pad(w, ((0, 0), (0, width - w.shape[1])))


def _layer_weights(l, w_in, ret_gn_g, dsa_qn_g, dsa_kn_g, idx_kn_g, mla_cqn_g, mla_ckvn_g, mla_krn_g,
                   mla_qn_g, w_uq, w_uk, w_uv, w_out, w_group, b_group, w_router, b_router, w1, w3, w2):
    offs = np.cumsum((0,) + IN_SPLITS)
    cols = [w_in[l][:, offs[k]:offs[k + 1]] for k in range(len(IN_SPLITS))]
    rq, rk, rv, rg, dq, dk, dv, iq, ik, iw, cq, ckv, kr = cols
    w_pack = jnp.concatenate([rq, rk, rv, rg, dq, dk, dv, iq, _pad_cols(ik, 128), cq, ckv,
                              _pad_cols(kr, 128), _pad_cols(iw, 128)], axis=1).astype(BF16)
    uq = w_uq[l].reshape(D_Q_LAT, H_MLA, D_QK_MLA)
    uq_pack = jnp.concatenate([uq[:, :, :D_NOPE].reshape(D_Q_LAT, 384), uq[:, :, D_NOPE:].reshape(D_Q_LAT, 192),
                               jnp.zeros((D_Q_LAT, 64), F32)], axis=1).astype(BF16)
    ukbd = jnp.zeros((384, 768), F32)
    for hh in range(H_MLA):
        ukbd = ukbd.at[64 * hh:64 * hh + 64, 128 * hh:128 * hh + 128].set(w_uk[l, hh])
    g_mq = jnp.concatenate([jnp.tile(mla_qn_g[l, :D_NOPE], H_MLA), jnp.tile(mla_qn_g[l, D_NOPE:], H_MLA),
                            jnp.zeros((64,), F32)])[None, :]
    head_of = np.concatenate([np.repeat(np.arange(H_MLA), D_NOPE), np.repeat(np.arange(H_MLA), D_ROPE),
                              np.full((64,), -1)])
    bq = jnp.asarray((head_of[:, None] == head_of[None, :]) & (head_of[:, None] >= 0), BF16)
    h64 = np.arange(256) // 64
    b64 = jnp.asarray(h64[:, None] == h64[None, :], BF16)
    w_rt = jnp.concatenate([w_router[l], w_group[l], jnp.zeros((D_MODEL, LANES - N_EXPERTS - N_GROUPS), F32)], axis=1)
    rt_hi = w_rt.astype(BF16)
    rt_lo = (w_rt - rt_hi.astype(F32)).astype(BF16)
    b_rt = jnp.concatenate([b_router[l], b_group[l], jnp.zeros((LANES - N_EXPERTS - N_GROUPS,), F32)])[None, :]
    return dict(
        w_pack=w_pack, w_uq=uq_pack, w_ukbd=ukbd.astype(BF16), b64=b64, bq=bq,
        g_dq=jnp.tile(dsa_qn_g[l], H_DSA)[None, :],
        g_kv=_pad_cols(dsa_kn_g[l][None, :], 128), g_ik=_pad_cols(idx_kn_g[l][None, :], 128),
        g_cq=mla_cqn_g[l][None, :], g_ckv=mla_ckvn_g[l][None, :], g_kr=_pad_cols(mla_krn_g[l][None, :], 128),
        g_mq=g_mq, gn=ret_gn_g[l].reshape(1, H_RET * DV_RET),
        wuvt=jnp.swapaxes(w_uv[l], 1, 2).astype(BF16),
        wuv_cat=jnp.transpose(w_uv[l], (1, 0, 2)).reshape(D_KV_LAT, H_MLA * DV_MLA).astype(BF16),
        w_out=w_out[l].astype(BF16), w_rt_hi=rt_hi, w_rt_lo=rt_lo, b_rt=b_rt,
        w13=jnp.concatenate([w1[l], w3[l]], axis=-1).astype(BF16), w2=w2[l].astype(BF16))


def _pick_tm(T, pref):
    tm = min(pref, T)
    while T % tm:
        tm //= 2
    return tm


def kernel(x_prompt, x_sample, cache_dsa, cache_mla, state_ret, page_table, c_prompt, c_sample, norm1_g, norm2_g, w_ada, b_ada, w_in, ret_gn_g, dsa_qn_g, dsa_kn_g, idx_kn_g, mla_cqn_g, mla_ckvn_g, mla_krn_g, mla_qn_g, w_uq, w_uk, w_uv, w_out, w_group, b_group, w_router, b_router, w1, w3, w2):
    depth = w_in.shape[0]
    BP, S, _ = x_prompt.shape
    B, T, _ = x_sample.shape
    assert BP == 1 and S % 256 == 0 and T == 8
    n_pages = page_table.shape[1]
    past = n_pages * PAGE
    topk_p = min(TOPK_MAX, S // 4)
    topk_s = min(TOPK_MAX, (past + T) // 4)
    log_gammas = tuple(float(np.log(np.float32(1.0) - np.float32(2.0) ** np.float32(-5.0 - h))) for h in range(H_RET))
    KC = _pick_tm(S, 512)
    assert S % (KC * UNROLL) == 0
    cache_dsa_t = jnp.swapaxes(cache_dsa, 2, 3)
    cache_mla_t = jnp.swapaxes(cache_mla, 2, 3)

    n_rows = 1 + B
    r_pad = (-n_rows) % 8
    c_all = jnp.concatenate([c_prompt, c_sample, jnp.zeros((r_pad, D_MODEL), F32)], axis=0)
    mods = _ada(c_all, w_ada, b_ada)

    tabs_p = _rope_tables(jnp.arange(S), 64) + _rope_tables(jnp.arange(S), 32)
    pos_s = jnp.tile(past + jnp.arange(T), B)
    tabs_s = _rope_tables(pos_s, 64) + _rope_tables(pos_s, 32)
    pt_flat = page_table.reshape(-1)

    xp = x_prompt.reshape(S, D_MODEL)
    xs = x_sample.reshape(B * T, D_MODEL)
    tm_p = KC
    tm_s = _pick_tm(B * T, 256)
    zero_state = jnp.zeros((1, H_RET, DK_RET, DV_RET), F32)
    outs = [[] for _ in range(6)]
    for l in range(depth):
        lw = _layer_weights(l, w_in, ret_gn_g, dsa_qn_g, dsa_kn_g, idx_kn_g, mla_cqn_g, mla_ckvn_g, mla_krn_g,
                            mla_qn_g, w_uq, w_uk, w_uv, w_out, w_group, b_group, w_router, b_router, w1, w3, w2)
        mp = [mods[l, 0:1, k * D_MODEL:(k + 1) * D_MODEL] for k in range(6)]
        ms = [jnp.repeat(mods[l, 1:1 + B, k * D_MODEL:(k + 1) * D_MODEL], T, axis=0) for k in range(6)]
        n1 = norm1_g[l][None, :]
        n2 = norm2_g[l][None, :]

        (rq, rk, rv, rg, dq_hm, iq_hm, _, iwt, drow, kd, ikd, kvt, qc_hm, mrow, kvm, ckvt) = _proj(
            xp, mp[0], mp[1], n1, tabs_p, lw, BF16, tm_p)
        o_ret, st_p = _retention(rq, rk, rv, rg, lw["gn"], zero_state, 128, log_gammas, BF16)
        o_dsa = _dsa_prompt(iq_hm, iwt, dq_hm, ikd, kd, kvt, topk_p, KC)
        o_mla = _mla_prompt(qc_hm, kvm, ckvt, lw["wuvt"], KC)
        x1, h2, gate = _mix(o_ret, o_dsa, o_mla, xp, mp[2], mp[3], mp[4], n2, lw, tm_p)
        xp = _moe(h2, gate, x1, mp[5], lw["w13"], lw["w2"], _pick_tm(S, 1024))
        outs[0].append(drow.reshape(1, S, DSA_ROW))
        outs[2].append(mrow.reshape(1, S, MLA_ROW))
        outs[4].append(st_p)

        (rq, rk, rv, rg, dq_hm, iq_hm, iw, _, drow, _, _, _, qc_hm, mrow, _, _) = _proj(
            xs, ms[0], ms[1], n1, tabs_s, lw, F32, tm_s)
        o_ret, st_s = _retention(rq, rk, rv, rg, lw["gn"], state_ret[l].astype(F32), T, log_gammas, F32)
        o_dsa = _dsa_sample(l, cache_dsa_t, pt_flat, n_pages, iq_hm, iw, dq_hm, drow, T, topk_s)
        o_mla = _mla_sample(l, cache_mla_t, pt_flat, n_pages, qc_hm, mrow, lw["wuv_cat"], T)
        x1, h2, gate = _mix(o_ret, o_dsa, o_mla, xs, ms[2], ms[3], ms[4], n2, lw, tm_s)
        xs = _moe(h2, gate, x1, ms[5], lw["w13"], lw["w2"], _pick_tm(B * T, 1024))
        outs[1].append(drow.reshape(B, T, DSA_ROW))
        outs[3].append(mrow.reshape(B, T, MLA_ROW))
        outs[5].append(st_s)

    return (xp.reshape(1, S, D_MODEL), xs.reshape(B, T, D_MODEL),
            jnp.stack(outs[0]), jnp.stack(outs[1]), jnp.stack(outs[2]), jnp.stack(outs[3]),
            jnp.stack(outs[4]), jnp.stack(outs[5]))
```

```python
import functools
import math

import numpy as np
import jax
import jax.numpy as jnp
from jax import lax
from jax.experimental import pallas as pl
from jax.experimental.pallas import tpu as pltpu

F32 = jnp.float32
BF16 = jnp.bfloat16
I32 = jnp.int32

D_MODEL = 1024
PAGE = 128
H_RET, DK_RET, DV_RET = 6, 64, 64
H_DSA, HD_DSA = 4, 64
H_IDX, D_IDX = 8, 64
TOPK_MAX = 256
H_MLA, D_Q_LAT, D_KV_LAT, D_NOPE, D_ROPE, DV_MLA = 6, 256, 128, 64, 32, 64
N_GROUPS, EPG, N_EXPERTS, D_EXPERT = 4, 8, 32, 256
Q_BLOCK = 128
ROPE_THETA = 10000.0
EPS = 1e-6
DSA_ROW = 2 * HD_DSA + D_IDX
MLA_ROW = D_KV_LAT + D_ROPE
D_QK_MLA = D_NOPE + D_ROPE
IN_SPLITS = (384, 384, 384, 384, 256, 64, 64, 512, 64, 8, 256, 128, 32)

C_RQ, C_RK, C_RV, C_RG = 0, 384, 768, 1152
C_DQ, C_KV, C_IQ, C_IK, C_CQ, C_CKV, C_KR, C_IW = 1536, 1792, 1920, 2432, 2560, 2816, 2944, 3072
D_PACK = 3200
MLA_QW = 640

LANES = 128
VMEM_LIMIT = 56 * 1024 * 1024

INT_MIN = -2 ** 31
NEG_BIG = -1e30
LOG2E = math.log2(math.e)
CKVT_ROWS = D_KV_LAT + 16
UNROLL = 2
KEY_NEG_INF = (0xFF800000 ^ 0x7FFFFFFF) - 2 ** 32


def _cparams(sem):
    return pltpu.CompilerParams(dimension_semantics=sem, vmem_limit_bytes=VMEM_LIMIT)


def _dot(a, b):
    return jnp.dot(a, b, preferred_element_type=F32)


def _dot_nt(a, b):
    return lax.dot_general(a, b, (((1,), (1,)), ((), ())), preferred_element_type=F32)


def _split(x):
    hi = x.astype(BF16)
    lo = (x - hi.astype(F32)).astype(BF16)
    return hi, lo


def _dot_f32_lhs(x, b_bf16):
    hi, lo = _split(x)
    return _dot(hi, b_bf16) + _dot(lo, b_bf16)


def _sigmoid(x):
    return 1.0 / (1.0 + jnp.exp(-x))


def _sort_key(x):
    b = lax.bitcast_convert_type(x, I32)
    return b ^ ((b >> 31) & 0x7FFFFFFF)


def _rope(x, cos, sin, half):
    w = x.shape[-1]
    fwd = pltpu.roll(x, w - half, axis=1)
    bwd = pltpu.roll(x, half, axis=1)
    lane = lax.broadcasted_iota(I32, x.shape, 1)
    first = (lane & (2 * half - 1)) < half
    return x * cos + jnp.where(first, fwd, bwd) * sin


def _tile_lanes(t, n):
    return t if n == 1 else jnp.concatenate([t] * n, axis=1)


def _ada_kernel(c_ref, w_ref, b_ref, o_ref):
    c = c_ref[...]
    a = c * _sigmoid(c)
    ah, al = _split(a)
    w = w_ref[0]
    wh, wl = _split(w)
    o_ref[0] = _dot(ah, wh) + _dot(ah, wl) + _dot(al, wh) + b_ref[0]


def _ada(c_all, w_ada, b_ada):
    L = w_ada.shape[0]
    R = c_all.shape[0]
    tn = 1024
    return pl.pallas_call(
        _ada_kernel,
        grid=(L, 6 * D_MODEL // tn),
        in_specs=[pl.BlockSpec((R, D_MODEL), lambda l, j: (0, 0)),
                  pl.BlockSpec((1, D_MODEL, tn), lambda l, j: (l, 0, j)),
                  pl.BlockSpec((1, 1, tn), lambda l, j: (l, 0, j))],
        out_specs=pl.BlockSpec((1, R, tn), lambda l, j: (l, 0, j)),
        out_shape=jax.ShapeDtypeStruct((L, R, 6 * D_MODEL), F32),
        compiler_params=_cparams(("arbitrary", "arbitrary")),
    )(c_all, w_ada, b_ada.reshape(L, 1, 6 * D_MODEL))


def _proj_kernel(x_ref, sh_ref, sc_ref, g1_ref, c64_ref, s64_ref, c32_ref, s32_ref,
                 w_ref, gq_ref, gkv_ref, gik_ref, gcq_ref, gckv_ref, gkr_ref, gmq_ref,
                 wuq_ref, wuk_ref, b64_ref, bq_ref,
                 rq_ref, rk_ref, rv_ref, rg_ref, dq_ref, iq_ref, iw_ref, iwt_ref,
                 drow_ref, kd_ref, ikd_ref, kvt_ref, qc_ref, mrow_ref, kvm_ref, ckvt_ref):
    x = x_ref[...]
    rstd = lax.rsqrt(jnp.mean(x * x, axis=-1, keepdims=True) + EPS)
    h = (x * rstd) * g1_ref[...] * (1.0 + sc_ref[...]) + sh_ref[...]
    hb = h.astype(BF16)
    c64, s64, c32, s32 = c64_ref[...], s64_ref[...], c32_ref[...], s32_ref[...]

    def zcols(c0, width):
        return _dot(hb, w_ref[:, c0:c0 + width])

    rq_ref[...] = _rope(zcols(C_RQ, 384), _tile_lanes(c64, 3), _tile_lanes(s64, 3), 32)
    rk_ref[...] = _rope(zcols(C_RK, 384), _tile_lanes(c64, 3), _tile_lanes(s64, 3), 32) * (DK_RET ** -0.5)
    rv_ref[...] = zcols(C_RV, 384)
    rg_ref[...] = zcols(C_RG, 384)

    dq = zcols(C_DQ, 256)
    ss = _dot_f32_lhs(dq * dq, b64_ref[...])
    dq = dq * lax.rsqrt(ss * (1.0 / HD_DSA) + EPS) * gq_ref[...]
    dq = _rope(dq, _tile_lanes(c64, 2), _tile_lanes(s64, 2), 32) * (HD_DSA ** -0.5 * LOG2E)
    for hh in range(H_DSA):
        dq_ref[hh] = dq[:, 64 * hh:64 * hh + 64].astype(dq_ref.dtype)

    kv = zcols(C_KV, 128)
    lane = lax.broadcasted_iota(I32, kv.shape, 1)
    is_k = lane < HD_DSA
    ssk = jnp.sum(jnp.where(is_k, kv * kv, 0.0), axis=-1, keepdims=True)
    kn = kv * lax.rsqrt(ssk * (1.0 / HD_DSA) + EPS) * gkv_ref[...]
    kv = jnp.where(is_k, _rope(kn, c64, s64, 32), kv)
    drow_ref[:, 0:128] = kv
    kvb = kv.astype(BF16)
    kd_ref[...] = kvb[:, 0:64]
    kvt_ref[0] = jnp.where(is_k, 1.0, kv).T.astype(BF16)

    iq = _rope(zcols(C_IQ, 512), _tile_lanes(c64, 4), _tile_lanes(s64, 4), 32) * (D_IDX ** -0.5)
    for hh in range(H_IDX):
        iq_ref[hh] = iq[:, 64 * hh:64 * hh + 64].astype(iq_ref.dtype)
    ik = zcols(C_IK, 128)
    ssi = jnp.sum(ik * ik, axis=-1, keepdims=True)
    ik = _rope(ik * lax.rsqrt(ssi * (1.0 / D_IDX) + EPS) * gik_ref[...], c64, s64, 32)
    drow_ref[:, 128:192] = ik[:, 0:64]
    ikd_ref[...] = ik[:, 0:64].astype(BF16)
    iw = zcols(C_IW, 128) * (H_IDX ** -0.5)
    iw_ref[...] = iw
    iwt_ref[...] = iw.T[0:H_IDX, :]

    cq = zcols(C_CQ, 256)
    cq = cq * lax.rsqrt(jnp.mean(cq * cq, axis=-1, keepdims=True) + EPS) * gcq_ref[...]
    q = _dot(cq.astype(BF16), wuq_ref[...])
    ssq = _dot_f32_lhs(q * q, bq_ref[...])
    q = q * lax.rsqrt(ssq * (1.0 / D_QK_MLA) + EPS) * gmq_ref[...]
    scale = D_QK_MLA ** -0.5 * LOG2E
    q_lat = _dot(q[:, 0:384].astype(BF16), wuk_ref[...]) * scale
    q_rope = _rope(q[:, 384:640], _tile_lanes(c32, 2), _tile_lanes(s32, 2), 16) * scale
    for hh in range(H_MLA):
        qc_ref[hh, :, 0:128] = q_lat[:, 128 * hh:128 * hh + 128].astype(qc_ref.dtype)
        qc_ref[hh, :, 128:160] = q_rope[:, 32 * hh:32 * hh + 32].astype(qc_ref.dtype)
    ckv = zcols(C_CKV, 128)
    ckv = ckv * lax.rsqrt(jnp.mean(ckv * ckv, axis=-1, keepdims=True) + EPS) * gckv_ref[...]
    kr = zcols(C_KR, 128)
    ssr = jnp.sum(kr * kr, axis=-1, keepdims=True)
    kr = _rope(kr * lax.rsqrt(ssr * (1.0 / D_ROPE) + EPS) * gkr_ref[...], c32, s32, 16)
    mrow_ref[:, 0:128] = ckv
    mrow_ref[:, 128:160] = kr[:, 0:32]
    kvm_ref[:, 0:128] = ckv.astype(BF16)
    kvm_ref[:, 128:160] = kr[:, 0:32].astype(BF16)
    ckvt_ref[0, 0:D_KV_LAT, :] = ckv.T.astype(BF16)
    ckvt_ref[0, D_KV_LAT:CKVT_ROWS, :] = jnp.ones((CKVT_ROWS - D_KV_LAT, ckv.shape[0]), BF16)


def _proj(x, shift, scale, g1, tabs, lw, qdtype, tm):
    T = x.shape[0]
    nb = T // tm
    per_tok = shift.shape[0] != 1

    def row(width):
        return pl.BlockSpec((tm, width), lambda i: (i, 0))

    def const(shape):
        nd = len(shape)
        return pl.BlockSpec(shape, lambda i: (0,) * nd)

    mod_spec = row(D_MODEL) if per_tok else const((1, D_MODEL))
    in_specs = [row(D_MODEL), mod_spec, mod_spec, const((1, D_MODEL)),
                row(LANES), row(LANES), row(LANES), row(LANES),
                const((D_MODEL, D_PACK)),
                const((1, 256)), const((1, 128)), const((1, 128)), const((1, 256)),
                const((1, 128)), const((1, 128)), const((1, MLA_QW)),
                const((D_Q_LAT, MLA_QW)), const((384, 768)), const((256, 256)), const((MLA_QW, MLA_QW))]
    out_shape = [
        jax.ShapeDtypeStruct((T, 384), F32), jax.ShapeDtypeStruct((T, 384), F32),
        jax.ShapeDtypeStruct((T, 384), F32), jax.ShapeDtypeStruct((T, 384), F32),
        jax.ShapeDtypeStruct((H_DSA, T, 64), qdtype), jax.ShapeDtypeStruct((H_IDX, T, 64), qdtype),
        jax.ShapeDtypeStruct((T, LANES), F32), jax.ShapeDtypeStruct((H_IDX, T), F32),
        jax.ShapeDtypeStruct((T, DSA_ROW), F32), jax.ShapeDtypeStruct((T, 64), BF16),
        jax.ShapeDtypeStruct((T, 64), BF16), jax.ShapeDtypeStruct((nb, 128, tm), BF16),
        jax.ShapeDtypeStruct((H_MLA, T, 160), qdtype), jax.ShapeDtypeStruct((T, MLA_ROW), F32),
        jax.ShapeDtypeStruct((T, MLA_ROW), BF16), jax.ShapeDtypeStruct((nb, CKVT_ROWS, tm), BF16)]
    out_specs = [
        row(384), row(384), row(384), row(384),
        pl.BlockSpec((H_DSA, tm, 64), lambda i: (0, i, 0)), pl.BlockSpec((H_IDX, tm, 64), lambda i: (0, i, 0)),
        row(LANES), pl.BlockSpec((H_IDX, tm), lambda i: (0, i)),
        row(DSA_ROW), row(64), row(64), pl.BlockSpec((1, 128, tm), lambda i: (i, 0, 0)),
        pl.BlockSpec((H_MLA, tm, 160), lambda i: (0, i, 0)), row(MLA_ROW), row(MLA_ROW),
        pl.BlockSpec((1, CKVT_ROWS, tm), lambda i: (i, 0, 0))]
    return pl.pallas_call(
        _proj_kernel, grid=(nb,), in_specs=in_specs, out_specs=out_specs, out_shape=out_shape,
        compiler_params=_cparams(("arbitrary",)),
    )(x, shift, scale, g1, *tabs, lw["w_pack"], lw["g_dq"], lw["g_kv"], lw["g_ik"], lw["g_cq"],
      lw["g_ckv"], lw["g_kr"], lw["g_mq"], lw["w_uq"], lw["w_ukbd"], lw["b64"], lw["bq"])


def _ret_kernel(log_gammas, C, q_ref, k_ref, v_ref, g_ref, gn_ref, s0_ref, o_ref, st_ref):
    first = pl.program_id(1) == 0

    @pl.when(first)
    def _():
        st_ref[...] = s0_ref[...]

    ri = lax.broadcasted_iota(I32, (C, C), 0)
    ci = lax.broadcasted_iota(I32, (C, C), 1)
    rel = (ri - ci).astype(F32)
    pos = lax.broadcasted_iota(I32, (C, 1), 0).astype(F32)
    q_all, k_all, v_all, g_all, gn = q_ref[...], k_ref[...], v_ref[...], g_ref[...], gn_ref[...]
    for hh in range(H_RET):
        lg = log_gammas[hh]
        sl = slice(64 * hh, 64 * hh + 64)
        q, k, v = q_all[:, sl], k_all[:, sl], v_all[:, sl]
        qb, vb = q.astype(BF16), v.astype(BF16)
        decay = jnp.where(rel >= 0, jnp.exp(lg * jnp.maximum(rel, 0.0)), 0.0)
        attn = _dot_nt(qb, k.astype(BF16)) * decay
        state = st_ref[0, hh]
        o = _dot(attn.astype(BF16), vb) + _dot(qb, state.astype(BF16)) * jnp.exp(lg * (pos + 1.0))
        k_dec = k * jnp.exp(lg * (C - 1.0 - pos))
        st_ref[0, hh] = state * math.exp(lg * C) + _dot(k_dec.T.astype(BF16), vb)
        mu = jnp.mean(o, axis=-1, keepdims=True)
        d = o - mu
        var = jnp.mean(d * d, axis=-1, keepdims=True)
        gate = g_all[:, sl]
        o_ref[:, sl] = (d * lax.rsqrt(var + EPS) * gn[:, sl] * (gate * _sigmoid(gate))).astype(o_ref.dtype)


def _retention(rq, rk, rv, rg, gn, state0, C, log_gammas, out_dtype):
    NB = state0.shape[0]
    T = rq.shape[0]
    nc = T // (NB * C)
    row = pl.BlockSpec((C, 384), lambda b, c: (b * nc + c, 0))
    st = pl.BlockSpec((1, H_RET, DK_RET, DV_RET), lambda b, c: (b, 0, 0, 0))
    return pl.pallas_call(
        functools.partial(_ret_kernel, log_gammas, C),
        grid=(NB, nc),
        in_specs=[row, row, row, row, pl.BlockSpec((1, 384), lambda b, c: (0, 0)), st],
        out_specs=[row, st],
        out_shape=[jax.ShapeDtypeStruct((T, 384), out_dtype),
                   jax.ShapeDtypeStruct((NB, H_RET, DK_RET, DV_RET), F32)],
        compiler_params=_cparams(("arbitrary", "arbitrary")),
    )(rq, rk, rv, rg, gn, state0)


def _count_ge(keys_ref, nch, kc, cand):
    def body(c, acc):
        blk = keys_ref[pl.ds(pl.multiple_of(c * kc, kc), kc), :]
        ind = jnp.where(blk >= cand, 1, 0).astype(I32)
        return acc + jnp.sum(ind.reshape(kc // 8, 8, LANES), axis=0)
    acc = lax.fori_loop(0, nch, body, jnp.zeros((8, LANES), I32))
    return jnp.sum(acc, axis=0, keepdims=True)


def _key_to_float(k):
    return lax.bitcast_convert_type(k ^ ((k >> 31) & 0x7FFFFFFF), F32)


def _key_range(fmin, fmax):
    kmin, kmax = _sort_key(fmin), _sort_key(fmax)
    return jnp.where(kmin == 0, -1, kmin), jnp.where(kmax == -1, 0, kmax)


def _kth_largest_key(count_fn, topk, n_valid, kmin, kmax):
    tgt = math.log2(topk)

    def lg(c):
        return jnp.log2(jnp.maximum(c.astype(F32), 0.5))

    def active(lo, hi, c_lo):
        return (c_lo > topk) & (hi > lo + 1)

    def cond(st):
        it, lo, hi, c_lo = st[:4]
        return (it < 128) & (jnp.max(jnp.where(active(lo, hi, c_lo), 1, 0)) > 0)

    def body(st):
        it, lo, hi, c_lo, l_lo, l_hi, side = st
        f_lo, f_hi = _key_to_float(lo), _key_to_float(hi)
        cand = _sort_key(f_lo + (f_hi - f_lo) * ((l_lo - tgt) / (l_lo - l_hi)))
        mid = (lo >> 1) + (hi >> 1) + (lo & hi & 1)
        cand = jnp.where((cand > lo) & (cand < hi), cand, mid)
        cand = jnp.where((it >= 12) & ((it & 1) == 0), mid, cand)
        c = count_fn(cand)
        act = active(lo, hi, c_lo)
        up = act & (c >= topk)
        dn = act & (c < topk)
        lc = lg(c)
        l_lo_n = jnp.where(up, lc, jnp.where(dn & (side < 0), tgt + 0.5 * (l_lo - tgt), l_lo))
        l_hi_n = jnp.where(dn, lc, jnp.where(up & (side > 0), tgt + 0.5 * (l_hi - tgt), l_hi))
        return (it + 1, jnp.where(up, cand, lo), jnp.where(dn, cand, hi), jnp.where(up, c, c_lo),
                l_lo_n, l_hi_n, jnp.where(up, 1, jnp.where(dn, -1, side)))
    init = (jnp.int32(0), kmin, kmax + 1, n_valid, lg(n_valid), jnp.full(kmin.shape, -1.0, F32),
            jnp.zeros(kmin.shape, I32))
    return lax.while_loop(cond, body, init)[1]


def _dsa_prompt_kernel(topk, kc, iq_ref, iwt_ref, dq_ref, ikd_ref, kd_ref, kvt_ref, o_ref, keys_ref):
    i = pl.program_id(0)
    sup = kc * UNROLL
    last = (i * Q_BLOCK) // sup
    iqs = iq_ref[...].reshape(H_IDX * Q_BLOCK, D_IDX)
    iwt = iwt_ref[...]

    def idx_scores(start):
        s = _dot_nt(ikd_ref[pl.ds(start, kc), :], iqs)
        score = jnp.maximum(s[:, 0:LANES], 0.0) * iwt[0:1, :]
        for hh in range(1, H_IDX):
            score = score + jnp.maximum(s[:, LANES * hh:LANES * hh + LANES], 0.0) * iwt[hh:hh + 1, :]
        return score

    def fold(x, op):
        return op(x.reshape(kc // 8, 8, LANES), axis=0)

    def idx_body(c, carry):
        mx, mn = carry
        for u in range(UNROLL):
            start = pl.multiple_of((c * UNROLL + u) * kc, kc)
            sc = idx_scores(start)
            keys_ref[pl.ds(start, kc), :] = _sort_key(sc)
            mx, mn = jnp.maximum(mx, fold(sc, jnp.max)), jnp.minimum(mn, fold(sc, jnp.min))
        return mx, mn
    mx, mn = lax.fori_loop(0, last, idx_body, (jnp.full((8, LANES), -jnp.inf, F32), jnp.full((8, LANES), jnp.inf, F32)))
    q_pos = i * Q_BLOCK + lax.broadcasted_iota(I32, (kc, LANES), 1)
    for u in range(UNROLL):
        start = pl.multiple_of((last * UNROLL + u) * kc, kc)
        valid = start + lax.broadcasted_iota(I32, (kc, LANES), 0) <= q_pos
        sc = idx_scores(start)
        sc_lo = jnp.where(valid, sc, -jnp.inf)
        keys_ref[pl.ds(start, kc), :] = _sort_key(sc_lo)
        mx = jnp.maximum(mx, fold(sc_lo, jnp.max))
        mn = jnp.minimum(mn, fold(jnp.where(valid, sc, jnp.inf), jnp.min))

    kmin, kmax = _key_range(jnp.min(mn, axis=0, keepdims=True), jnp.max(mx, axis=0, keepdims=True))
    n_valid = i * Q_BLOCK + lax.broadcasted_iota(I32, (1, LANES), 1) + 1
    thr = _kth_largest_key(lambda cand: _count_ge(keys_ref, (last + 1) * UNROLL, kc, cand), topk, n_valid, kmin, kmax)

    dqs = dq_ref[...].reshape(H_DSA * Q_BLOCK, HD_DSA)
    W = H_DSA * LANES

    def att_step(cs, carry):
        m, acc = carry
        start = pl.multiple_of(cs * kc, kc)
        s = _dot_nt(kd_ref[pl.ds(start, kc), :], dqs)
        sel = keys_ref[pl.ds(start, kc), :] >= thr
        s = jnp.concatenate([jnp.where(sel, s[:, LANES * hh:LANES * hh + LANES], NEG_BIG)
                             for hh in range(H_DSA)], axis=1)
        m_new = jnp.maximum(m, jnp.max(s, axis=0, keepdims=True))
        p = jnp.exp2(s - m_new)
        acc = jnp.exp2(m - m_new) * acc + _dot(kvt_ref[cs], p.astype(BF16))
        return m_new, acc

    def att_body(c, carry):
        for u in range(UNROLL):
            carry = att_step(c * UNROLL + u, carry)
        return carry
    m0 = jnp.full((1, W), NEG_BIG, F32)
    m, acc = lax.fori_loop(0, last + 1, att_body, (m0, jnp.zeros((LANES, W), F32)))
    o_t = acc[HD_DSA:, :] / acc[0:1, :]
    o_hq = jnp.concatenate([o_t[:, LANES * hh:LANES * hh + LANES] for hh in range(H_DSA)], axis=0)
    o_ref[...] = o_hq.T.astype(o_ref.dtype)


def _dsa_prompt(iq_hm, iwt, dq_hm, ikd, kd, kvt, topk, kc):
    T = ikd.shape[0]
    nq = T // Q_BLOCK
    nkb = kvt.shape[0]
    return pl.pallas_call(
        functools.partial(_dsa_prompt_kernel, topk, kc),
        grid=(nq,),
        in_specs=[pl.BlockSpec((H_IDX, Q_BLOCK, D_IDX), lambda i: (0, i, 0)),
                  pl.BlockSpec((H_IDX, Q_BLOCK), lambda i: (0, i)),
                  pl.BlockSpec((H_DSA, Q_BLOCK, HD_DSA), lambda i: (0, i, 0)),
                  pl.BlockSpec((T, D_IDX), lambda i: (0, 0)),
                  pl.BlockSpec((T, HD_DSA), lambda i: (0, 0)),
                  pl.BlockSpec((nkb, 128, kc), lambda i: (0, 0, 0))],
        out_specs=pl.BlockSpec((Q_BLOCK, H_DSA * HD_DSA), lambda i: (i, 0)),
        out_shape=jax.ShapeDtypeStruct((T, H_DSA * HD_DSA), BF16),
        scratch_shapes=[pltpu.VMEM((nkb * kc, LANES), I32)],
        compiler_params=_cparams(("arbitrary",)),
    )(iq_hm, iwt, dq_hm, ikd, kd, kvt)


def _mla_prompt_kernel(kc, qc_ref, kvm_ref, ckvt_ref, wuvt_ref, o_ref):
    i = pl.program_id(0)
    last = (i * Q_BLOCK) // (kc * UNROLL)
    W = H_MLA * LANES
    qs = qc_ref[...].reshape(H_MLA * Q_BLOCK, MLA_ROW)

    def step(c, carry, causal):
        m, acc = carry
        start = pl.multiple_of(c * kc, kc)
        s = _dot_nt(kvm_ref[pl.ds(start, kc), :], qs)
        if causal:
            q_pos = i * Q_BLOCK + (lax.broadcasted_iota(I32, (kc, W), 1) & (LANES - 1))
            key_pos = start + lax.broadcasted_iota(I32, (kc, W), 0)
            s = jnp.where(key_pos <= q_pos, s, NEG_BIG)
        m_new = jnp.maximum(m, jnp.max(s, axis=0, keepdims=True))
        p = jnp.exp2(s - m_new)
        acc = jnp.exp2(m - m_new) * acc + _dot(ckvt_ref[c], p.astype(BF16))
        return m_new, acc
    carry = (jnp.full((1, W), NEG_BIG, F32), jnp.zeros((CKVT_ROWS, W), F32))

    def body(c, cr):
        for u in range(UNROLL):
            cr = step(c * UNROLL + u, cr, False)
        return cr
    carry = lax.fori_loop(0, last, body, carry)
    for u in range(UNROLL):
        carry = step(last * UNROLL + u, carry, True)
    m, acc = carry
    o_lat = (acc[0:D_KV_LAT, :] / acc[D_KV_LAT:D_KV_LAT + 1, :]).astype(BF16)
    outs = [_dot(wuvt_ref[hh], o_lat[:, LANES * hh:LANES * hh + LANES]) for hh in range(H_MLA)]
    o_ref[...] = jnp.concatenate(outs, axis=0).T.astype(o_ref.dtype)


def _mla_prompt(qc_hm, kvm, ckvt, wuvt, kc):
    T = kvm.shape[0]
    nq = T // Q_BLOCK
    nkb = ckvt.shape[0]
    return pl.pallas_call(
        functools.partial(_mla_prompt_kernel, kc),
        grid=(nq,),
        in_specs=[pl.BlockSpec((H_MLA, Q_BLOCK, MLA_ROW), lambda i: (0, i, 0)),
                  pl.BlockSpec((T, MLA_ROW), lambda i: (0, 0)),
                  pl.BlockSpec((nkb, CKVT_ROWS, kc), lambda i: (0, 0, 0)),
                  pl.BlockSpec((H_MLA, DV_MLA, D_KV_LAT), lambda i: (0, 0, 0))],
        out_specs=pl.BlockSpec((Q_BLOCK, H_MLA * DV_MLA), lambda i: (i, 0)),
        out_shape=jax.ShapeDtypeStruct((T, H_MLA * DV_MLA), BF16),
        compiler_params=_cparams(("arbitrary",)),
    )(qc_hm, kvm, ckvt, wuvt)


def _dsa_sample_kernel(topk, NP, NJ, T, *refs):
    pages = refs[1:1 + NP]
    iq_ref, iw_ref, dq_ref, new_ref, o_ref, rowst_ref, newp_ref, keys_ref, satt_ref, mm_ref = refs[1 + NP:]
    j = pl.program_id(1)
    CH = NP * PAGE
    iqs = iq_ref[...].reshape(H_IDX * T, D_IDX).astype(BF16)
    dqs = dq_ref[...].reshape(H_DSA * T, HD_DSA).astype(BF16)
    iw = iw_ref[...]

    def head_sum(s):
        score = jnp.maximum(s[0:T, :], 0.0) * iw[:, 0:1]
        for hh in range(1, H_IDX):
            score = score + jnp.maximum(s[T * hh:T * hh + T, :], 0.0) * iw[:, hh:hh + 1]
        return score

    def per_head(sel, s):
        return jnp.concatenate([jnp.where(sel, s[T * hh:T * hh + T, :], NEG_BIG) for hh in range(H_DSA)], axis=0)

    chunk = jnp.concatenate([pages[p][...] for p in range(NP)], axis=1).astype(BF16)
    rowst_ref[j] = chunk
    sc = head_sum(_dot(iqs, chunk[2 * HD_DSA:DSA_ROW, :]))
    keys_ref[j] = _sort_key(sc)
    satt_ref[j] = _dot(dqs, chunk[0:HD_DSA, :])

    def fold(x, op):
        return functools.reduce(op, [x[:, LANES * g:LANES * g + LANES] for g in range(x.shape[1] // LANES)])

    @pl.when(j == 0)
    def _():
        mm_ref[0] = jnp.full((T, LANES), -jnp.inf, F32)
        mm_ref[1] = jnp.full((T, LANES), jnp.inf, F32)
    mm_ref[0] = jnp.maximum(mm_ref[0], fold(sc, jnp.maximum))
    mm_ref[1] = jnp.minimum(mm_ref[1], fold(sc, jnp.minimum))

    @pl.when(j == NJ - 1)
    def _():
        newp_ref[...] = jnp.zeros((PAGE, DSA_ROW), BF16)
        newp_ref[0:T, :] = new_ref[...].astype(BF16)
        newp = newp_ref[...]
        qi = lax.broadcasted_iota(I32, (T, PAGE), 0)
        kj = lax.broadcasted_iota(I32, (T, PAGE), 1)
        idx_new = head_sum(_dot_nt(iqs, newp[:, 2 * HD_DSA:DSA_ROW]))
        sc_new = jnp.where(kj <= qi, idx_new, -jnp.inf)
        keys_new = _sort_key(sc_new)
        fmax = jnp.max(jnp.maximum(mm_ref[0], sc_new), axis=1, keepdims=True)
        fmin = jnp.min(jnp.minimum(mm_ref[1], jnp.where(kj <= qi, idx_new, jnp.inf)), axis=1, keepdims=True)
        kmin, kmax = _key_range(fmin, fmax)
        n_valid = NJ * CH + 1 + lax.broadcasted_iota(I32, (T, 1), 0)
        s_new = _dot_nt(dqs, newp[:, 0:HD_DSA])

        def count_fn(cand):
            accs = [jnp.where(keys_new >= cand, 1, 0).astype(I32)] + [jnp.zeros((T, LANES), I32)] * 3
            for c in range(NJ):
                ind = jnp.where(keys_ref[c] >= cand, 1, 0).astype(I32)
                for g in range(CH // LANES):
                    accs[g % 4] = accs[g % 4] + ind[:, LANES * g:LANES * g + LANES]
            return jnp.sum((accs[0] + accs[1]) + (accs[2] + accs[3]), axis=1, keepdims=True)
        thr = _kth_largest_key(count_fn, topk, n_valid, kmin, kmax)

        sn = per_head(keys_new >= thr, s_new)
        m = jnp.max(sn, axis=1, keepdims=True)
        for c in range(NJ):
            m = jnp.maximum(m, jnp.max(per_head(keys_ref[c] >= thr, satt_ref[c]), axis=1, keepdims=True))
        pn = jnp.exp2(sn - m)
        l = jnp.sum(pn, axis=1, keepdims=True)
        acc = _dot(pn.astype(BF16), newp[:, HD_DSA:2 * HD_DSA])
        for c in range(NJ):
            p = jnp.exp2(per_head(keys_ref[c] >= thr, satt_ref[c]) - m)
            l = l + jnp.sum(p, axis=1, keepdims=True)
            acc = acc + _dot_nt(p.astype(BF16), rowst_ref[c, HD_DSA:2 * HD_DSA, :])
        o = acc / l
        for hh in range(H_DSA):
            o_ref[:, 64 * hh:64 * hh + 64] = o[T * hh:T * hh + T, :]


def _dsa_sample(l, cache, pt_flat, n_pages, iq_hm, iw, dq_hm, new_rows, T, topk):
    B = pt_flat.shape[0] // n_pages
    NP = min(16, n_pages)
    NJ = n_pages // NP
    CH = NP * PAGE

    def page_spec(p):
        return pl.BlockSpec((None, None, DSA_ROW, PAGE),
                            lambda b, j, pt: (l, pt[b * n_pages + j * NP + p], 0, 0))
    in_specs = [page_spec(p) for p in range(NP)] + [
        pl.BlockSpec((H_IDX, T, D_IDX), lambda b, j, pt: (0, b, 0)),
        pl.BlockSpec((T, LANES), lambda b, j, pt: (b, 0)),
        pl.BlockSpec((H_DSA, T, HD_DSA), lambda b, j, pt: (0, b, 0)),
        pl.BlockSpec((T, DSA_ROW), lambda b, j, pt: (b, 0))]
    grid_spec = pltpu.PrefetchScalarGridSpec(
        num_scalar_prefetch=1, grid=(B, NJ), in_specs=in_specs,
        out_specs=pl.BlockSpec((T, H_DSA * HD_DSA), lambda b, j, pt: (b, 0)),
        scratch_shapes=[pltpu.VMEM((NJ, DSA_ROW, CH), BF16), pltpu.VMEM((PAGE, DSA_ROW), BF16),
                        pltpu.VMEM((NJ, T, CH), I32), pltpu.VMEM((NJ, H_DSA * T, CH), F32),
                        pltpu.VMEM((2, T, LANES), F32)])
    return pl.pallas_call(
        functools.partial(_dsa_sample_kernel, topk, NP, NJ, T),
        grid_spec=grid_spec,
        out_shape=jax.ShapeDtypeStruct((B * T, H_DSA * HD_DSA), F32),
        compiler_params=_cparams(("arbitrary", "arbitrary")),
    )(pt_flat, *([cache] * NP), iq_hm, iw, dq_hm, new_rows)


def _mla_sample_kernel(NP, NJ, T, *refs):
    pages = refs[1:1 + NP]
    qc_ref, new_ref, wuv_ref, o_ref, newp_ref, lat_ref, s_ref, mx_ref = refs[1 + NP:]
    j = pl.program_id(1)
    R = H_MLA * T
    CH = NP * PAGE
    qs = qc_ref[...].reshape(R, MLA_ROW).astype(BF16)

    chunk = jnp.concatenate([pages[p][...] for p in range(NP)], axis=1).astype(BF16)
    lat_ref[j] = chunk[0:D_KV_LAT, :]
    s = _dot(qs, chunk)
    s_ref[j] = s
    smax = functools.reduce(jnp.maximum, [s[:, LANES * g:LANES * g + LANES] for g in range(CH // LANES)])

    @pl.when(j == 0)
    def _():
        mx_ref[...] = smax

    @pl.when(j > 0)
    def _():
        mx_ref[...] = jnp.maximum(mx_ref[...], smax)

    @pl.when(j == NJ - 1)
    def _():
        newp_ref[...] = jnp.zeros((PAGE, MLA_ROW), BF16)
        newp_ref[0:T, :] = new_ref[...].astype(BF16)
        newp = newp_ref[...]
        qi = lax.broadcasted_iota(I32, (R, PAGE), 0) & (T - 1)
        kj = lax.broadcasted_iota(I32, (R, PAGE), 1)
        s_new = jnp.where(kj <= qi, _dot_nt(qs, newp), NEG_BIG)
        m = jnp.max(jnp.maximum(mx_ref[...], s_new), axis=1, keepdims=True)
        p_new = jnp.exp2(s_new - m)
        l = jnp.sum(p_new, axis=1, keepdims=True)
        acc = _dot(p_new.astype(BF16), newp[:, 0:D_KV_LAT])
        for c in range(NJ):
            p = jnp.exp2(s_ref[c] - m)
            l = l + jnp.sum(p, axis=1, keepdims=True)
            acc = acc + _dot_nt(p.astype(BF16), lat_ref[c])
        o_lat = (acc / l).astype(BF16)
        o_all = _dot(o_lat, wuv_ref[...])
        lane_head = lax.broadcasted_iota(I32, (T, H_MLA * DV_MLA), 1) // DV_MLA
        out = jnp.zeros((T, H_MLA * DV_MLA), F32)
        for hh in range(H_MLA):
            out = out + jnp.where(lane_head == hh, o_all[T * hh:T * hh + T, :], 0.0)
        o_ref[...] = out


def _mla_sample(l, cache, pt_flat, n_pages, qc_hm, new_rows, wuv_cat, T):
    B = pt_flat.shape[0] // n_pages
    NP = min(16, n_pages)
    NJ = n_pages // NP
    R = H_MLA * T

    def page_spec(p):
        return pl.BlockSpec((None, None, MLA_ROW, PAGE),
                            lambda b, j, pt: (l, pt[b * n_pages + j * NP + p], 0, 0))
    in_specs = [page_spec(p) for p in range(NP)] + [
        pl.BlockSpec((H_MLA, T, MLA_ROW), lambda b, j, pt: (0, b, 0)),
        pl.BlockSpec((T, MLA_ROW), lambda b, j, pt: (b, 0)),
        pl.BlockSpec((D_KV_LAT, H_MLA * DV_MLA), lambda b, j, pt: (0, 0))]
    grid_spec = pltpu.PrefetchScalarGridSpec(
        num_scalar_prefetch=1, grid=(B, NJ), in_specs=in_specs,
        out_specs=pl.BlockSpec((T, H_MLA * DV_MLA), lambda b, j, pt: (b, 0)),
        scratch_shapes=[pltpu.VMEM((PAGE, MLA_ROW), BF16), pltpu.VMEM((NJ, D_KV_LAT, NP * PAGE), BF16),
                        pltpu.VMEM((NJ, R, NP * PAGE), F32), pltpu.VMEM((R, LANES), F32)])
    return pl.pallas_call(
        functools.partial(_mla_sample_kernel, NP, NJ, T),
        grid_spec=grid_spec,
        out_shape=jax.ShapeDtypeStruct((B * T, H_MLA * DV_MLA), F32),
        compiler_params=_cparams(("arbitrary", "arbitrary")),
    )(pt_flat, *([cache] * NP), qc_hm, new_rows, wuv_cat)


def _mix_kernel(oret_ref, odsa_ref, omla_ref, x_ref, g1_ref, sh2_ref, sc2_ref, n2_ref,
                wout_ref, wrh_ref, wrl_ref, br_ref, x1_ref, h2_ref, gate_ref):
    y = (_dot(oret_ref[...].astype(BF16), wout_ref[0:384, :])
         + _dot(odsa_ref[...].astype(BF16), wout_ref[384:640, :])
         + _dot(omla_ref[...].astype(BF16), wout_ref[640:1024, :]))
    x1 = x_ref[...] + g1_ref[...] * y
    x1_ref[...] = x1
    rstd = lax.rsqrt(jnp.mean(x1 * x1, axis=-1, keepdims=True) + EPS)
    h2 = (x1 * rstd) * n2_ref[...] * (1.0 + sc2_ref[...]) + sh2_ref[...]
    h2b = h2.astype(BF16)
    h2_ref[...] = h2b
    logits = _dot(h2b, wrh_ref[...]) + _dot(h2b, wrl_ref[...]) + br_ref[...]
    lane = lax.broadcasted_iota(I32, logits.shape, 1)
    is_g = (lane >= N_EXPERTS) & (lane < N_EXPERTS + N_GROUPS)
    gl = jnp.where(is_g, logits, -jnp.inf)
    gmax = jnp.max(gl, axis=-1, keepdims=True)
    grp = jnp.min(jnp.where(gl == gmax, lane - N_EXPERTS, N_GROUPS), axis=-1, keepdims=True)
    g_prob = 1.0 / jnp.sum(jnp.exp(gl - gmax), axis=-1, keepdims=True)
    in_grp = (lane < N_EXPERTS) & ((lane >> 3) == grp)
    el = jnp.where(in_grp, logits, -jnp.inf)
    m1 = jnp.max(el, axis=-1, keepdims=True)
    i1 = jnp.min(jnp.where(el == m1, lane, LANES), axis=-1, keepdims=True)
    el2 = jnp.where(lane == i1, -jnp.inf, el)
    m2 = jnp.max(el2, axis=-1, keepdims=True)
    i2 = jnp.min(jnp.where(el2 == m2, lane, LANES), axis=-1, keepdims=True)
    e2 = jnp.exp(m2 - m1)
    den = 1.0 + e2
    gate_ref[...] = jnp.where(lane == i1, 1.0 / den, jnp.where(lane == i2, e2 / den, 0.0)) * g_prob


def _mix(o_ret, o_dsa, o_mla, x, gate1, shift2, scale2, n2, lw, tm):
    T = x.shape[0]
    per_tok = gate1.shape[0] != 1

    def row(width):
        return pl.BlockSpec((tm, width), lambda i: (i, 0))

    def const(shape):
        return pl.BlockSpec(shape, lambda i: (0, 0))
    mod_spec = row(D_MODEL) if per_tok else const((1, D_MODEL))
    return pl.pallas_call(
        _mix_kernel, grid=(T // tm,),
        in_specs=[row(384), row(256), row(384), row(D_MODEL), mod_spec, mod_spec, mod_spec,
                  const((1, D_MODEL)), const((D_MODEL, D_MODEL)), const((D_MODEL, LANES)),
                  const((D_MODEL, LANES)), const((1, LANES))],
        out_specs=[row(D_MODEL), row(D_MODEL), row(LANES)],
        out_shape=[jax.ShapeDtypeStruct((T, D_MODEL), F32), jax.ShapeDtypeStruct((T, D_MODEL), BF16),
                   jax.ShapeDtypeStruct((T, LANES), F32)],
        compiler_params=_cparams(("arbitrary",)),
    )(o_ret, o_dsa, o_mla, x, gate1, shift2, scale2, n2, lw["w_out"], lw["w_rt_hi"], lw["w_rt_lo"], lw["b_rt"])


def _moe_kernel(h_ref, gate_ref, x_ref, g2_ref, w13_ref, w2_ref, o_ref, acc_ref):
    e = pl.program_id(1)

    @pl.when(e == 0)
    def _():
        acc_ref[...] = jnp.zeros_like(acc_ref)

    gate = gate_ref[...]
    lane = lax.broadcasted_iota(I32, gate.shape, 1)
    g_col = jnp.sum(jnp.where(lane == e, gate, 0.0), axis=-1, keepdims=True)
    au = _dot(h_ref[...], w13_ref[0])
    a, u = au[:, 0:D_EXPERT], au[:, D_EXPERT:]
    act = (a * _sigmoid(a)) * u * g_col
    acc_ref[...] += _dot(act.astype(BF16), w2_ref[0])

    @pl.when(e == pl.num_programs(1) - 1)
    def _():
        o_ref[...] = x_ref[...] + g2_ref[...] * acc_ref[...]


def _moe(h2, gate, x1, gate2, w13, w2, tm):
    T = h2.shape[0]
    per_tok = gate2.shape[0] != 1
    mod_spec = (pl.BlockSpec((tm, D_MODEL), lambda i, e: (i, 0)) if per_tok
                else pl.BlockSpec((1, D_MODEL), lambda i, e: (0, 0)))
    return pl.pallas_call(
        _moe_kernel, grid=(T // tm, N_EXPERTS),
        in_specs=[pl.BlockSpec((tm, D_MODEL), lambda i, e: (i, 0)),
                  pl.BlockSpec((tm, LANES), lambda i, e: (i, 0)),
                  pl.BlockSpec((tm, D_MODEL), lambda i, e: (i, 0)),
                  mod_spec,
                  pl.BlockSpec((1, D_MODEL, 2 * D_EXPERT), lambda i, e: (e, 0, 0)),
                  pl.BlockSpec((1, D_EXPERT, D_MODEL), lambda i, e: (e, 0, 0))],
        out_specs=pl.BlockSpec((tm, D_MODEL), lambda i, e: (i, 0)),
        out_shape=jax.ShapeDtypeStruct((T, D_MODEL), F32),
        scratch_shapes=[pltpu.VMEM((tm, D_MODEL), F32)],
        compiler_params=_cparams(("arbitrary", "arbitrary")),
    )(h2, gate, x1, gate2, w13, w2)


def _rope_tables(pos, d):
    half = d // 2
    inv = ROPE_THETA ** (-jnp.arange(half, dtype=F32) * 2.0 / d)
    ang = pos.astype(F32)[:, None] * inv[None, :]
    cos, sin = jnp.cos(ang), jnp.sin(ang)
    reps = LANES // d
    return (jnp.tile(jnp.concatenate([cos, cos], axis=1), (1, reps)),
            jnp.tile(jnp.concatenate([-sin, sin], axis=1), (1, reps)))


def _pad_cols(w, width):
    return jnp.pad(w, ((0, 0), (0, width - w.shape[1])))


def _layer_weights(l, w_in, ret_gn_g, dsa_qn_g, dsa_kn_g, idx_kn_g, mla_cqn_g, mla_ckvn_g, mla_krn_g,
                   mla_qn_g, w_uq, w_uk, w_uv, w_out, w_group, b_group, w_router, b_router, w1, w3, w2):
    offs = np.cumsum((0,) + IN_SPLITS)
    cols = [w_in[l][:, offs[k]:offs[k + 1]] for k in range(len(IN_SPLITS))]
    rq, rk, rv, rg, dq, dk, dv, iq, ik, iw, cq, ckv, kr = cols
    w_pack = jnp.concatenate([rq, rk, rv, rg, dq, dk, dv, iq, _pad_cols(ik, 128), cq, ckv,
                              _pad_cols(kr, 128), _pad_cols(iw, 128)], axis=1).astype(BF16)
    uq = w_uq[l].reshape(D_Q_LAT, H_MLA, D_QK_MLA)
    uq_pack = jnp.concatenate([uq[:, :, :D_NOPE].reshape(D_Q_LAT, 384), uq[:, :, D_NOPE:].reshape(D_Q_LAT, 192),
                               jnp.zeros((D_Q_LAT, 64), F32)], axis=1).astype(BF16)
    ukbd = jnp.zeros((384, 768), F32)
    for hh in range(H_MLA):
        ukbd = ukbd.at[64 * hh:64 * hh + 64, 128 * hh:128 * hh + 128].set(w_uk[l, hh])
    g_mq = jnp.concatenate([jnp.tile(mla_qn_g[l, :D_NOPE], H_MLA), jnp.tile(mla_qn_g[l, D_NOPE:], H_MLA),
                            jnp.zeros((64,), F32)])[None, :]
    head_of = np.concatenate([np.repeat(np.arange(H_MLA), D_NOPE), np.repeat(np.arange(H_MLA), D_ROPE),
                              np.full((64,), -1)])
    bq = jnp.asarray((head_of[:, None] == head_of[None, :]) & (head_of[:, None] >= 0), BF16)
    h64 = np.arange(256) // 64
    b64 = jnp.asarray(h64[:, None] == h64[None, :], BF16)
    w_rt = jnp.concatenate([w_router[l], w_group[l], jnp.zeros((D_MODEL, LANES - N_EXPERTS - N_GROUPS), F32)], axis=1)
    rt_hi = w_rt.astype(BF16)
    rt_lo = (w_rt - rt_hi.astype(F32)).astype(BF16)
    b_rt = jnp.concatenate([b_router[l], b_group[l], jnp.zeros((LANES - N_EXPERTS - N_GROUPS,), F32)])[None, :]
    return dict(
        w_pack=w_pack, w_uq=uq_pack, w_ukbd=ukbd.astype(BF16), b64=b64, bq=bq,
        g_dq=jnp.tile(dsa_qn_g[l], H_DSA)[None, :],
        g_kv=_pad_cols(dsa_kn_g[l][None, :], 128), g_ik=_pad_cols(idx_kn_g[l][None, :], 128),
        g_cq=mla_cqn_g[l][None, :], g_ckv=mla_ckvn_g[l][None, :], g_kr=_pad_cols(mla_krn_g[l][None, :], 128),
        g_mq=g_mq, gn=ret_gn_g[l].reshape(1, H_RET * DV_RET),
        wuvt=jnp.swapaxes(w_uv[l], 1, 2).astype(BF16),
        wuv_cat=jnp.transpose(w_uv[l], (1, 0, 2)).reshape(D_KV_LAT, H_MLA * DV_MLA).astype(BF16),
        w_out=w_out[l].astype(BF16), w_rt_hi=rt_hi, w_rt_lo=rt_lo, b_rt=b_rt,
        w13=jnp.concatenate([w1[l], w3[l]], axis=-1).astype(BF16), w2=w2[l].astype(BF16))


def _pick_tm(T, pref):
    tm = min(pref, T)
    while T % tm:
        tm //= 2
    return tm


def kernel(x_prompt, x_sample, cache_dsa, cache_mla, state_ret, page_table, c_prompt, c_sample, norm1_g, norm2_g, w_ada, b_ada, w_in, ret_gn_g, dsa_qn_g, dsa_kn_g, idx_kn_g, mla_cqn_g, mla_ckvn_g, mla_krn_g, mla_qn_g, w_uq, w_uk, w_uv, w_out, w_group, b_group, w_router, b_router, w1, w3, w2):
    depth = w_in.shape[0]
    BP, S, _ = x_prompt.shape
    B, T, _ = x_sample.shape
    assert BP == 1 and S % 256 == 0 and T == 8
    n_pages = page_table.shape[1]
    past = n_pages * PAGE
    topk_p = min(TOPK_MAX, S // 4)
    topk_s = min(TOPK_MAX, (past + T) // 4)
    log_gammas = tuple(float(np.log(np.float32(1.0) - np.float32(2.0) ** np.float32(-5.0 - h))) for h in range(H_RET))
    KC = _pick_tm(S, 512)
    assert S % (KC * UNROLL) == 0
    cache_dsa_t = jnp.swapaxes(cache_dsa, 2, 3)
    cache_mla_t = jnp.swapaxes(cache_mla, 2, 3)

    n_rows = 1 + B
    r_pad = (-n_rows) % 8
    c_all = jnp.concatenate([c_prompt, c_sample, jnp.zeros((r_pad, D_MODEL), F32)], axis=0)
    mods = _ada(c_all, w_ada, b_ada)

    tabs_p = _rope_tables(jnp.arange(S), 64) + _rope_tables(jnp.arange(S), 32)
    pos_s = jnp.tile(past + jnp.arange(T), B)
    tabs_s = _rope_tables(pos_s, 64) + _rope_tables(pos_s, 32)
    pt_flat = page_table.reshape(-1)

    xp = x_prompt.reshape(S, D_MODEL)
    xs = x_sample.reshape(B * T, D_MODEL)
    tm_p = KC
    tm_s = _pick_tm(B * T, 256)
    zero_state = jnp.zeros((1, H_RET, DK_RET, DV_RET), F32)
    outs = [[] for _ in range(6)]
    for l in range(depth):
        lw = _layer_weights(l, w_in, ret_gn_g, dsa_qn_g, dsa_kn_g, idx_kn_g, mla_cqn_g, mla_ckvn_g, mla_krn_g,
                            mla_qn_g, w_uq, w_uk, w_uv, w_out, w_group, b_group, w_router, b_router, w1, w3, w2)
        mp = [mods[l, 0:1, k * D_MODEL:(k + 1) * D_MODEL] for k in range(6)]
        ms = [jnp.repeat(mods[l, 1:1 + B, k * D_MODEL:(k + 1) * D_MODEL], T, axis=0) for k in range(6)]
        n1 = norm1_g[l][None, :]
        n2 = norm2_g[l][None, :]

        (rq, rk, rv, rg, dq_hm, iq_hm, _, iwt, drow, kd, ikd, kvt, qc_hm, mrow, kvm, ckvt) = _proj(
            xp, mp[0], mp[1], n1, tabs_p, lw, BF16, tm_p)
        o_ret, st_p = _retention(rq, rk, rv, rg, lw["gn"], zero_state, 128, log_gammas, BF16)
        o_dsa = _dsa_prompt(iq_hm, iwt, dq_hm, ikd, kd, kvt, topk_p, KC)
        o_mla = _mla_prompt(qc_hm, kvm, ckvt, lw["wuvt"], KC)
        x1, h2, gate = _mix(o_ret, o_dsa, o_mla, xp, mp[2], mp[3], mp[4], n2, lw, tm_p)
        xp = _moe(h2, gate, x1, mp[5], lw["w13"], lw["w2"], _pick_tm(S, 1024))
        outs[0].append(drow.reshape(1, S, DSA_ROW))
        outs[2].append(mrow.reshape(1, S, MLA_ROW))
        outs[4].append(st_p)

        (rq, rk, rv, rg, dq_hm, iq_hm, iw, _, drow, _, _, _, qc_hm, mrow, _, _) = _proj(
            xs, ms[0], ms[1], n1, tabs_s, lw, F32, tm_s)
        o_ret, st_s = _retention(rq, rk, rv, rg, lw["gn"], state_ret[l].astype(F32), T, log_gammas, F32)
        o_dsa = _dsa_sample(l, cache_dsa_t, pt_flat, n_pages, iq_hm, iw, dq_hm, drow, T, topk_s)
        o_mla = _mla_sample(l, cache_mla_t, pt_flat, n_pages, qc_hm, mrow, lw["wuv_cat"], T)
        x1, h2, gate = _mix(o_ret, o_dsa, o_mla, xs, ms[2], ms[3], ms[4], n2, lw, tm_s)
        xs = _moe(h2, gate, x1, ms[5], lw["w13"], lw["w2"], _pick_tm(B * T, 1024))
        outs[1].append(drow.reshape(B, T, DSA_ROW))
        outs[3].append(mrow.reshape(B, T, MLA_ROW))
        outs[5].append(st_s)

    return (xp.reshape(1, S, D_MODEL), xs.reshape(B, T, D_MODEL),
            jnp.stack(outs[0]), jnp.stack(outs[1]), jnp.stack(outs[2]), jnp.stack(outs[3]),
            jnp.stack(outs[4]), jnp.stack(outs[5]))
```

```python
import functools
import math

import numpy as np
import jax
import jax.numpy as jnp
from jax import lax
from jax.experimental import pallas as pl
from jax.experimental.pallas import tpu as pltpu

F32 = jnp.float32
BF16 = jnp.bfloat16
I32 = jnp.int32

D_MODEL = 1024
PAGE = 128
H_RET, DK_RET, DV_RET = 6, 64, 64
H_DSA, HD_DSA = 4, 64
H_IDX, D_IDX = 8, 64
TOPK_MAX = 256
H_MLA, D_Q_LAT, D_KV_LAT, D_NOPE, D_ROPE, DV_MLA = 6, 256, 128, 64, 32, 64
N_GROUPS, EPG, N_EXPERTS, D_EXPERT = 4, 8, 32, 256
Q_BLOCK = 128
ROPE_THETA = 10000.0
EPS = 1e-6
DSA_ROW = 2 * HD_DSA + D_IDX
MLA_ROW = D_KV_LAT + D_ROPE
D_QK_MLA = D_NOPE + D_ROPE
IN_SPLITS = (384, 384, 384, 384, 256, 64, 64, 512, 64, 8, 256, 128, 32)

C_RQ, C_RK, C_RV, C_RG = 0, 384, 768, 1152
C_DQ, C_KV, C_IQ, C_IK, C_CQ, C_CKV, C_KR, C_IW = 1536, 1792, 1920, 2432, 2560, 2816, 2944, 3072
D_PACK = 3200
MLA_QW = 640

LANES = 128
VMEM_LIMIT = 56 * 1024 * 1024

INT_MIN = -2 ** 31
NEG_BIG = -1e30
LOG2E = math.log2(math.e)
CKVT_ROWS = D_KV_LAT + 16
UNROLL = 2
ZERO_ATOM = 1 << 15
KEY_NEG_INF = (0xFF800000 ^ 0x7FFFFFFF) - 2 ** 32


def _cparams(sem):
    return pltpu.CompilerParams(dimension_semantics=sem, vmem_limit_bytes=VMEM_LIMIT)


def _dot(a, b):
    return jnp.dot(a, b, preferred_element_type=F32)


def _dot_nt(a, b):
    return lax.dot_general(a, b, (((1,), (1,)), ((), ())), preferred_element_type=F32)


def _split(x):
    hi = x.astype(BF16)
    lo = (x - hi.astype(F32)).astype(BF16)
    return hi, lo


def _dot_f32_lhs(x, b_bf16):
    hi, lo = _split(x)
    return _dot(hi, b_bf16) + _dot(lo, b_bf16)


def _sigmoid(x):
    return 1.0 / (1.0 + jnp.exp(-x))


def _sort_key(x):
    b = lax.bitcast_convert_type(x, I32)
    return b ^ ((b >> 31) & 0x7FFFFFFF)


def _rope(x, cos, sin, half):
    w = x.shape[-1]
    fwd = pltpu.roll(x, w - half, axis=1)
    bwd = pltpu.roll(x, half, axis=1)
    lane = lax.broadcasted_iota(I32, x.shape, 1)
    first = (lane & (2 * half - 1)) < half
    return x * cos + jnp.where(first, fwd, bwd) * sin


def _tile_lanes(t, n):
    return t if n == 1 else jnp.concatenate([t] * n, axis=1)


def _ada_kernel(c_ref, w_ref, b_ref, o_ref):
    c = c_ref[...]
    a = c * _sigmoid(c)
    ah, al = _split(a)
    w = w_ref[0]
    wh, wl = _split(w)
    o_ref[0] = _dot(ah, wh) + _dot(ah, wl) + _dot(al, wh) + b_ref[0]


def _ada(c_all, w_ada, b_ada):
    L = w_ada.shape[0]
    R = c_all.shape[0]
    tn = 1024
    return pl.pallas_call(
        _ada_kernel,
        grid=(L, 6 * D_MODEL // tn),
        in_specs=[pl.BlockSpec((R, D_MODEL), lambda l, j: (0, 0)),
                  pl.BlockSpec((1, D_MODEL, tn), lambda l, j: (l, 0, j)),
                  pl.BlockSpec((1, 1, tn), lambda l, j: (l, 0, j))],
        out_specs=pl.BlockSpec((1, R, tn), lambda l, j: (l, 0, j)),
        out_shape=jax.ShapeDtypeStruct((L, R, 6 * D_MODEL), F32),
        compiler_params=_cparams(("arbitrary", "arbitrary")),
    )(c_all, w_ada, b_ada.reshape(L, 1, 6 * D_MODEL))


def _proj_kernel(x_ref, sh_ref, sc_ref, g1_ref, c64_ref, s64_ref, c32_ref, s32_ref,
                 w_ref, gq_ref, gkv_ref, gik_ref, gcq_ref, gckv_ref, gkr_ref, gmq_ref,
                 wuq_ref, wuk_ref, b64_ref, bq_ref,
                 rq_ref, rk_ref, rv_ref, rg_ref, dq_ref, iq_ref, iw_ref, iwt_ref,
                 drow_ref, kd_ref, ikd_ref, kvt_ref, qc_ref, mrow_ref, kvm_ref, ckvt_ref):
    x = x_ref[...]
    rstd = lax.rsqrt(jnp.mean(x * x, axis=-1, keepdims=True) + EPS)
    h = (x * rstd) * g1_ref[...] * (1.0 + sc_ref[...]) + sh_ref[...]
    hb = h.astype(BF16)
    c64, s64, c32, s32 = c64_ref[...], s64_ref[...], c32_ref[...], s32_ref[...]

    def zcols(c0, width):
        return _dot(hb, w_ref[:, c0:c0 + width])

    rq_ref[...] = _rope(zcols(C_RQ, 384), _tile_lanes(c64, 3), _tile_lanes(s64, 3), 32)
    rk_ref[...] = _rope(zcols(C_RK, 384), _tile_lanes(c64, 3), _tile_lanes(s64, 3), 32) * (DK_RET ** -0.5)
    rv_ref[...] = zcols(C_RV, 384)
    rg_ref[...] = zcols(C_RG, 384)

    dq = zcols(C_DQ, 256)
    ss = _dot_f32_lhs(dq * dq, b64_ref[...])
    dq = dq * lax.rsqrt(ss * (1.0 / HD_DSA) + EPS) * gq_ref[...]
    dq = _rope(dq, _tile_lanes(c64, 2), _tile_lanes(s64, 2), 32) * (HD_DSA ** -0.5 * LOG2E)
    for hh in range(H_DSA):
        dq_ref[hh] = dq[:, 64 * hh:64 * hh + 64].astype(dq_ref.dtype)

    kv = zcols(C_KV, 128)
    lane = lax.broadcasted_iota(I32, kv.shape, 1)
    is_k = lane < HD_DSA
    ssk = jnp.sum(jnp.where(is_k, kv * kv, 0.0), axis=-1, keepdims=True)
    kn = kv * lax.rsqrt(ssk * (1.0 / HD_DSA) + EPS) * gkv_ref[...]
    kv = jnp.where(is_k, _rope(kn, c64, s64, 32), kv)
    drow_ref[:, 0:128] = kv
    kvb = kv.astype(BF16)
    kd_ref[...] = kvb[:, 0:64]
    kvt_ref[0] = jnp.where(is_k, 1.0, kv).T.astype(BF16)

    iq = _rope(zcols(C_IQ, 512), _tile_lanes(c64, 4), _tile_lanes(s64, 4), 32) * (D_IDX ** -0.5)
    for hh in range(H_IDX):
        iq_ref[hh] = iq[:, 64 * hh:64 * hh + 64].astype(iq_ref.dtype)
    ik = zcols(C_IK, 128)
    ssi = jnp.sum(ik * ik, axis=-1, keepdims=True)
    ik = _rope(ik * lax.rsqrt(ssi * (1.0 / D_IDX) + EPS) * gik_ref[...], c64, s64, 32)
    drow_ref[:, 128:192] = ik[:, 0:64]
    ikd_ref[...] = ik[:, 0:64].astype(BF16)
    iw = zcols(C_IW, 128) * (H_IDX ** -0.5)
    iw_ref[...] = iw
    iwt_ref[...] = iw.T[0:H_IDX, :]

    cq = zcols(C_CQ, 256)
    cq = cq * lax.rsqrt(jnp.mean(cq * cq, axis=-1, keepdims=True) + EPS) * gcq_ref[...]
    q = _dot(cq.astype(BF16), wuq_ref[...])
    ssq = _dot_f32_lhs(q * q, bq_ref[...])
    q = q * lax.rsqrt(ssq * (1.0 / D_QK_MLA) + EPS) * gmq_ref[...]
    scale = D_QK_MLA ** -0.5 * LOG2E
    q_lat = _dot(q[:, 0:384].astype(BF16), wuk_ref[...]) * scale
    q_rope = _rope(q[:, 384:640], _tile_lanes(c32, 2), _tile_lanes(s32, 2), 16) * scale
    for hh in range(H_MLA):
        qc_ref[hh, :, 0:128] = q_lat[:, 128 * hh:128 * hh + 128].astype(qc_ref.dtype)
        qc_ref[hh, :, 128:160] = q_rope[:, 32 * hh:32 * hh + 32].astype(qc_ref.dtype)
    ckv = zcols(C_CKV, 128)
    ckv = ckv * lax.rsqrt(jnp.mean(ckv * ckv, axis=-1, keepdims=True) + EPS) * gckv_ref[...]
    kr = zcols(C_KR, 128)
    ssr = jnp.sum(kr * kr, axis=-1, keepdims=True)
    kr = _rope(kr * lax.rsqrt(ssr * (1.0 / D_ROPE) + EPS) * gkr_ref[...], c32, s32, 16)
    mrow_ref[:, 0:128] = ckv
    mrow_ref[:, 128:160] = kr[:, 0:32]
    kvm_ref[:, 0:128] = ckv.astype(BF16)
    kvm_ref[:, 128:160] = kr[:, 0:32].astype(BF16)
    ckvt_ref[0, 0:D_KV_LAT, :] = ckv.T.astype(BF16)
    ckvt_ref[0, D_KV_LAT:CKVT_ROWS, :] = jnp.ones((CKVT_ROWS - D_KV_LAT, ckv.shape[0]), BF16)


def _proj(x, shift, scale, g1, tabs, lw, qdtype, tm):
    T = x.shape[0]
    nb = T // tm
    per_tok = shift.shape[0] != 1

    def row(width):
        return pl.BlockSpec((tm, width), lambda i: (i, 0))

    def const(shape):
        nd = len(shape)
        return pl.BlockSpec(shape, lambda i: (0,) * nd)

    mod_spec = row(D_MODEL) if per_tok else const((1, D_MODEL))
    in_specs = [row(D_MODEL), mod_spec, mod_spec, const((1, D_MODEL)),
                row(LANES), row(LANES), row(LANES), row(LANES),
                const((D_MODEL, D_PACK)),
                const((1, 256)), const((1, 128)), const((1, 128)), const((1, 256)),
                const((1, 128)), const((1, 128)), const((1, MLA_QW)),
                const((D_Q_LAT, MLA_QW)), const((384, 768)), const((256, 256)), const((MLA_QW, MLA_QW))]
    out_shape = [
        jax.ShapeDtypeStruct((T, 384), F32), jax.ShapeDtypeStruct((T, 384), F32),
        jax.ShapeDtypeStruct((T, 384), F32), jax.ShapeDtypeStruct((T, 384), F32),
        jax.ShapeDtypeStruct((H_DSA, T, 64), qdtype), jax.ShapeDtypeStruct((H_IDX, T, 64), qdtype),
        jax.ShapeDtypeStruct((T, LANES), F32), jax.ShapeDtypeStruct((H_IDX, T), F32),
        jax.ShapeDtypeStruct((T, DSA_ROW), F32), jax.ShapeDtypeStruct((T, 64), BF16),
        jax.ShapeDtypeStruct((T, 64), BF16), jax.ShapeDtypeStruct((nb, 128, tm), BF16),
        jax.ShapeDtypeStruct((H_MLA, T, 160), qdtype), jax.ShapeDtypeStruct((T, MLA_ROW), F32),
        jax.ShapeDtypeStruct((T, MLA_ROW), BF16), jax.ShapeDtypeStruct((nb, CKVT_ROWS, tm), BF16)]
    out_specs = [
        row(384), row(384), row(384), row(384),
        pl.BlockSpec((H_DSA, tm, 64), lambda i: (0, i, 0)), pl.BlockSpec((H_IDX, tm, 64), lambda i: (0, i, 0)),
        row(LANES), pl.BlockSpec((H_IDX, tm), lambda i: (0, i)),
        row(DSA_ROW), row(64), row(64), pl.BlockSpec((1, 128, tm), lambda i: (i, 0, 0)),
        pl.BlockSpec((H_MLA, tm, 160), lambda i: (0, i, 0)), row(MLA_ROW), row(MLA_ROW),
        pl.BlockSpec((1, CKVT_ROWS, tm), lambda i: (i, 0, 0))]
    return pl.pallas_call(
        _proj_kernel, grid=(nb,), in_specs=in_specs, out_specs=out_specs, out_shape=out_shape,
        compiler_params=_cparams(("arbitrary",)),
    )(x, shift, scale, g1, *tabs, lw["w_pack"], lw["g_dq"], lw["g_kv"], lw["g_ik"], lw["g_cq"],
      lw["g_ckv"], lw["g_kr"], lw["g_mq"], lw["w_uq"], lw["w_ukbd"], lw["b64"], lw["bq"])


def _ret_kernel(log_gammas, C, q_ref, k_ref, v_ref, g_ref, gn_ref, s0_ref, o_ref, st_ref):
    first = pl.program_id(1) == 0

    @pl.when(first)
    def _():
        st_ref[...] = s0_ref[...]

    ri = lax.broadcasted_iota(I32, (C, C), 0)
    ci = lax.broadcasted_iota(I32, (C, C), 1)
    rel = (ri - ci).astype(F32)
    pos = lax.broadcasted_iota(I32, (C, 1), 0).astype(F32)
    q_all, k_all, v_all, g_all, gn = q_ref[...], k_ref[...], v_ref[...], g_ref[...], gn_ref[...]
    for hh in range(H_RET):
        lg = log_gammas[hh]
        sl = slice(64 * hh, 64 * hh + 64)
        q, k, v = q_all[:, sl], k_all[:, sl], v_all[:, sl]
        qb, vb = q.astype(BF16), v.astype(BF16)
        decay = jnp.where(rel >= 0, jnp.exp(lg * jnp.maximum(rel, 0.0)), 0.0)
        attn = _dot_nt(qb, k.astype(BF16)) * decay
        state = st_ref[0, hh]
        o = _dot(attn.astype(BF16), vb) + _dot(qb, state.astype(BF16)) * jnp.exp(lg * (pos + 1.0))
        k_dec = k * jnp.exp(lg * (C - 1.0 - pos))
        st_ref[0, hh] = state * math.exp(lg * C) + _dot(k_dec.T.astype(BF16), vb)
        mu = jnp.mean(o, axis=-1, keepdims=True)
        d = o - mu
        var = jnp.mean(d * d, axis=-1, keepdims=True)
        gate = g_all[:, sl]
        o_ref[:, sl] = (d * lax.rsqrt(var + EPS) * gn[:, sl] * (gate * _sigmoid(gate))).astype(o_ref.dtype)


def _retention(rq, rk, rv, rg, gn, state0, C, log_gammas, out_dtype):
    NB = state0.shape[0]
    T = rq.shape[0]
    nc = T // (NB * C)
    row = pl.BlockSpec((C, 384), lambda b, c: (b * nc + c, 0))
    st = pl.BlockSpec((1, H_RET, DK_RET, DV_RET), lambda b, c: (b, 0, 0, 0))
    return pl.pallas_call(
        functools.partial(_ret_kernel, log_gammas, C),
        grid=(NB, nc),
        in_specs=[row, row, row, row, pl.BlockSpec((1, 384), lambda b, c: (0, 0)), st],
        out_specs=[row, st],
        out_shape=[jax.ShapeDtypeStruct((T, 384), out_dtype),
                   jax.ShapeDtypeStruct((NB, H_RET, DK_RET, DV_RET), F32)],
        compiler_params=_cparams(("arbitrary", "arbitrary")),
    )(rq, rk, rv, rg, gn, state0)


def _count_ge(keys_ref, nch, kc, cand):
    def body(c, acc):
        blk = keys_ref[pl.ds(pl.multiple_of(c * kc, kc), kc), :]
        ind = jnp.where(blk >= cand, 1, 0).astype(I32)
        return acc + jnp.sum(ind.reshape(kc // 8, 8, LANES), axis=0)
    acc = lax.fori_loop(0, nch, body, jnp.zeros((8, LANES), I32))
    return jnp.sum(acc, axis=0, keepdims=True)


def _index_key(score, pos):
    k = _sort_key(score)
    return jnp.where(k > 0, k + ZERO_ATOM, jnp.where(k >= -1, ZERO_ATOM - pos, k))


def _index_key_of_float(x):
    k = _sort_key(x)
    return jnp.where(k > 0, k + ZERO_ATOM, k)


def _index_key_to_float(k):
    k = jnp.where(k > ZERO_ATOM, k - ZERO_ATOM, jnp.where(k > 0, 0, k))
    return lax.bitcast_convert_type(k ^ ((k >> 31) & 0x7FFFFFFF), F32)


def _index_key_range(fmin, fmax):
    kmin, kmax = _sort_key(fmin), _sort_key(fmax)
    return (jnp.where(kmin > 0, kmin + ZERO_ATOM, jnp.where(kmin >= -1, 1, kmin)),
            jnp.where(kmax > 0, kmax + ZERO_ATOM, jnp.where(kmax >= -1, ZERO_ATOM, kmax)))


def _kth_largest_key(count_fn, topk, n_valid, kmin, kmax):
    tgt = math.log2(topk)

    def lg(c):
        return jnp.log2(jnp.maximum(c.astype(F32), 0.5))

    def active(lo, hi, c_lo):
        return (c_lo > topk) & (hi > lo + 1)

    def cond(st):
        it, lo, hi, c_lo = st[:4]
        return (it < 128) & (jnp.max(jnp.where(active(lo, hi, c_lo), 1, 0)) > 0)

    def body(st):
        it, lo, hi, c_lo, c_hi, l_lo, l_hi, side = st
        f_lo, f_hi = _index_key_to_float(lo), _index_key_to_float(hi)
        by_score = _index_key_of_float(f_lo + (f_hi - f_lo) * ((l_lo - tgt) / (l_lo - l_hi)))
        frac = (c_lo - topk).astype(F32) / jnp.maximum(c_lo - c_hi, 1).astype(F32)
        by_pos = lo + 1 + ((hi - lo).astype(F32) * frac).astype(I32)
        cand = jnp.where((lo >= 1) & (hi <= ZERO_ATOM + 1), by_pos, by_score)
        cand = jnp.where(it == 0, ZERO_ATOM + 1, jnp.where(it == 1, 1, cand))
        mid = (lo >> 1) + (hi >> 1) + (lo & hi & 1)
        cand = jnp.where((cand > lo) & (cand < hi), cand, mid)
        cand = jnp.where((it >= 14) & ((it & 1) == 0), mid, cand)
        c = count_fn(cand)
        act = active(lo, hi, c_lo)
        up = act & (c >= topk)
        dn = act & (c < topk)
        lc = lg(c)
        l_lo_n = jnp.where(up, lc, jnp.where(dn & (side < 0), tgt + 0.5 * (l_lo - tgt), l_lo))
        l_hi_n = jnp.where(dn, lc, jnp.where(up & (side > 0), tgt + 0.5 * (l_hi - tgt), l_hi))
        return (it + 1, jnp.where(up, cand, lo), jnp.where(dn, cand, hi), jnp.where(up, c, c_lo),
                jnp.where(dn, c, c_hi), l_lo_n, l_hi_n, jnp.where(up, 1, jnp.where(dn, -1, side)))
    zero = jnp.zeros(kmin.shape, I32)
    init = (jnp.int32(0), kmin, kmax + 1, n_valid, zero, lg(n_valid), jnp.full(kmin.shape, -1.0, F32), zero)
    return lax.while_loop(cond, body, init)[1]


def _dsa_prompt_kernel(topk, kc, iq_ref, iwt_ref, dq_ref, ikd_ref, kd_ref, kvt_ref, o_ref, keys_ref):
    i = pl.program_id(0)
    sup = kc * UNROLL
    last = (i * Q_BLOCK) // sup
    iqs = iq_ref[...].reshape(H_IDX * Q_BLOCK, D_IDX)
    iwt = iwt_ref[...]

    def idx_scores(start):
        s = _dot_nt(ikd_ref[pl.ds(start, kc), :], iqs)
        score = jnp.maximum(s[:, 0:LANES], 0.0) * iwt[0:1, :]
        for hh in range(1, H_IDX):
            score = score + jnp.maximum(s[:, LANES * hh:LANES * hh + LANES], 0.0) * iwt[hh:hh + 1, :]
        return score

    def fold(x, op):
        return op(x.reshape(kc // 8, 8, LANES), axis=0)

    def idx_body(c, carry):
        mx, mn = carry
        for u in range(UNROLL):
            start = pl.multiple_of((c * UNROLL + u) * kc, kc)
            sc = idx_scores(start)
            keys_ref[pl.ds(start, kc), :] = _index_key(sc, start + row_iota)
            mx, mn = jnp.maximum(mx, fold(sc, jnp.max)), jnp.minimum(mn, fold(sc, jnp.min))
        return mx, mn
    row_iota = lax.broadcasted_iota(I32, (kc, LANES), 0)
    mx, mn = lax.fori_loop(0, last, idx_body, (jnp.full((8, LANES), -jnp.inf, F32), jnp.full((8, LANES), jnp.inf, F32)))
    q_pos = i * Q_BLOCK + lax.broadcasted_iota(I32, (kc, LANES), 1)
    for u in range(UNROLL):
        start = pl.multiple_of((last * UNROLL + u) * kc, kc)
        valid = start + row_iota <= q_pos
        sc = idx_scores(start)
        sc_lo = jnp.where(valid, sc, -jnp.inf)
        keys_ref[pl.ds(start, kc), :] = _index_key(sc_lo, start + row_iota)
        mx = jnp.maximum(mx, fold(sc_lo, jnp.max))
        mn = jnp.minimum(mn, fold(jnp.where(valid, sc, jnp.inf), jnp.min))

    kmin, kmax = _index_key_range(jnp.min(mn, axis=0, keepdims=True), jnp.max(mx, axis=0, keepdims=True))
    n_valid = i * Q_BLOCK + lax.broadcasted_iota(I32, (1, LANES), 1) + 1
    thr = _kth_largest_key(lambda cand: _count_ge(keys_ref, (last + 1) * UNROLL, kc, cand), topk, n_valid, kmin, kmax)

    dqs = dq_ref[...].reshape(H_DSA * Q_BLOCK, HD_DSA)
    W = H_DSA * LANES

    def att_step(cs, carry):
        m, acc = carry
        start = pl.multiple_of(cs * kc, kc)
        s = _dot_nt(kd_ref[pl.ds(start, kc), :], dqs)
        sel = keys_ref[pl.ds(start, kc), :] >= thr
        s = jnp.concatenate([jnp.where(sel, s[:, LANES * hh:LANES * hh + LANES], NEG_BIG)
                             for hh in range(H_DSA)], axis=1)
        m_new = jnp.maximum(m, jnp.max(s, axis=0, keepdims=True))
        p = jnp.exp2(s - m_new)
        acc = jnp.exp2(m - m_new) * acc + _dot(kvt_ref[cs], p.astype(BF16))
        return m_new, acc

    def att_body(c, carry):
        for u in range(UNROLL):
            carry = att_step(c * UNROLL + u, carry)
        return carry
    m0 = jnp.full((1, W), NEG_BIG, F32)
    m, acc = lax.fori_loop(0, last + 1, att_body, (m0, jnp.zeros((LANES, W), F32)))
    o_t = acc[HD_DSA:, :] / acc[0:1, :]
    o_hq = jnp.concatenate([o_t[:, LANES * hh:LANES * hh + LANES] for hh in range(H_DSA)], axis=0)
    o_ref[...] = o_hq.T.astype(o_ref.dtype)


def _dsa_prompt(iq_hm, iwt, dq_hm, ikd, kd, kvt, topk, kc):
    T = ikd.shape[0]
    nq = T // Q_BLOCK
    nkb = kvt.shape[0]
    return pl.pallas_call(
        functools.partial(_dsa_prompt_kernel, topk, kc),
        grid=(nq,),
        in_specs=[pl.BlockSpec((H_IDX, Q_BLOCK, D_IDX), lambda i: (0, i, 0)),
                  pl.BlockSpec((H_IDX, Q_BLOCK), lambda i: (0, i)),
                  pl.BlockSpec((H_DSA, Q_BLOCK, HD_DSA), lambda i: (0, i, 0)),
                  pl.BlockSpec((T, D_IDX), lambda i: (0, 0)),
                  pl.BlockSpec((T, HD_DSA), lambda i: (0, 0)),
                  pl.BlockSpec((nkb, 128, kc), lambda i: (0, 0, 0))],
        out_specs=pl.BlockSpec((Q_BLOCK, H_DSA * HD_DSA), lambda i: (i, 0)),
        out_shape=jax.ShapeDtypeStruct((T, H_DSA * HD_DSA), BF16),
        scratch_shapes=[pltpu.VMEM((nkb * kc, LANES), I32)],
        compiler_params=_cparams(("arbitrary",)),
    )(iq_hm, iwt, dq_hm, ikd, kd, kvt)


def _mla_prompt_kernel(kc, qc_ref, kvm_ref, ckvt_ref, wuvt_ref, o_ref):
    i = pl.program_id(0)
    last = (i * Q_BLOCK) // (kc * UNROLL)
    W = H_MLA * LANES
    qs = qc_ref[...].reshape(H_MLA * Q_BLOCK, MLA_ROW)

    def step(c, carry, causal):
        m, acc = carry
        start = pl.multiple_of(c * kc, kc)
        s = _dot_nt(kvm_ref[pl.ds(start, kc), :], qs)
        if causal:
            q_pos = i * Q_BLOCK + (lax.broadcasted_iota(I32, (kc, W), 1) & (LANES - 1))
            key_pos = start + lax.broadcasted_iota(I32, (kc, W), 0)
            s = jnp.where(key_pos <= q_pos, s, NEG_BIG)
        m_new = jnp.maximum(m, jnp.max(s, axis=0, keepdims=True))
        p = jnp.exp2(s - m_new)
        acc = jnp.exp2(m - m_new) * acc + _dot(ckvt_ref[c], p.astype(BF16))
        return m_new, acc
    carry = (jnp.full((1, W), NEG_BIG, F32), jnp.zeros((CKVT_ROWS, W), F32))

    def body(c, cr):
        for u in range(UNROLL):
            cr = step(c * UNROLL + u, cr, False)
        return cr
    carry = lax.fori_loop(0, last, body, carry)
    for u in range(UNROLL):
        carry = step(last * UNROLL + u, carry, True)
    m, acc = carry
    o_lat = (acc[0:D_KV_LAT, :] / acc[D_KV_LAT:D_KV_LAT + 1, :]).astype(BF16)
    outs = [_dot(wuvt_ref[hh], o_lat[:, LANES * hh:LANES * hh + LANES]) for hh in range(H_MLA)]
    o_ref[...] = jnp.concatenate(outs, axis=0).T.astype(o_ref.dtype)


def _mla_prompt(qc_hm, kvm, ckvt, wuvt, kc):
    T = kvm.shape[0]
    nq = T // Q_BLOCK
    nkb = ckvt.shape[0]
    return pl.pallas_call(
        functools.partial(_mla_prompt_kernel, kc),
        grid=(nq,),
        in_specs=[pl.BlockSpec((H_MLA, Q_BLOCK, MLA_ROW), lambda i: (0, i, 0)),
                  pl.BlockSpec((T, MLA_ROW), lambda i: (0, 0)),
                  pl.BlockSpec((nkb, CKVT_ROWS, kc), lambda i: (0, 0, 0)),
                  pl.BlockSpec((H_MLA, DV_MLA, D_KV_LAT), lambda i: (0, 0, 0))],
        out_specs=pl.BlockSpec((Q_BLOCK, H_MLA * DV_MLA), lambda i: (i, 0)),
        out_shape=jax.ShapeDtypeStruct((T, H_MLA * DV_MLA), BF16),
        compiler_params=_cparams(("arbitrary",)),
    )(qc_hm, kvm, ckvt, wuvt)


def _dsa_sample_kernel(topk, NP, NJ, T, *refs):
    pages = refs[1:1 + NP]
    iq_ref, iw_ref, dq_ref, new_ref, o_ref, rowst_ref, newp_ref, keys_ref, satt_ref, mm_ref = refs[1 + NP:]
    j = pl.program_id(1)
    CH = NP * PAGE
    iqs = iq_ref[...].reshape(H_IDX * T, D_IDX).astype(BF16)
    dqs = dq_ref[...].reshape(H_DSA * T, HD_DSA).astype(BF16)
    iw = iw_ref[...]

    def head_sum(s):
        score = jnp.maximum(s[0:T, :], 0.0) * iw[:, 0:1]
        for hh in range(1, H_IDX):
            score = score + jnp.maximum(s[T * hh:T * hh + T, :], 0.0) * iw[:, hh:hh + 1]
        return score

    def per_head(sel, s):
        return jnp.concatenate([jnp.where(sel, s[T * hh:T * hh + T, :], NEG_BIG) for hh in range(H_DSA)], axis=0)

    chunk = jnp.concatenate([pages[p][...] for p in range(NP)], axis=1).astype(BF16)
    rowst_ref[j] = chunk
    sc = head_sum(_dot(iqs, chunk[2 * HD_DSA:DSA_ROW, :]))
    keys_ref[j] = _index_key(sc, j * CH + lax.broadcasted_iota(I32, (T, CH), 1))
    satt_ref[j] = _dot(dqs, chunk[0:HD_DSA, :])

    def fold(x, op):
        return functools.reduce(op, [x[:, LANES * g:LANES * g + LANES] for g in range(x.shape[1] // LANES)])

    @pl.when(j == 0)
    def _():
        mm_ref[0] = jnp.full((T, LANES), -jnp.inf, F32)
        mm_ref[1] = jnp.full((T, LANES), jnp.inf, F32)
    mm_ref[0] = jnp.maximum(mm_ref[0], fold(sc, jnp.maximum))
    mm_ref[1] = jnp.minimum(mm_ref[1], fold(sc, jnp.minimum))

    @pl.when(j == NJ - 1)
    def _():
        newp_ref[...] = jnp.zeros((PAGE, DSA_ROW), BF16)
        newp_ref[0:T, :] = new_ref[...].astype(BF16)
        newp = newp_ref[...]
        qi = lax.broadcasted_iota(I32, (T, PAGE), 0)
        kj = lax.broadcasted_iota(I32, (T, PAGE), 1)
        idx_new = head_sum(_dot_nt(iqs, newp[:, 2 * HD_DSA:DSA_ROW]))
        sc_new = jnp.where(kj <= qi, idx_new, -jnp.inf)
        keys_new = _index_key(sc_new, NJ * CH + kj)
        fmax = jnp.max(jnp.maximum(mm_ref[0], sc_new), axis=1, keepdims=True)
        fmin = jnp.min(jnp.minimum(mm_ref[1], jnp.where(kj <= qi, idx_new, jnp.inf)), axis=1, keepdims=True)
        kmin, kmax = _index_key_range(fmin, fmax)
        n_valid = NJ * CH + 1 + lax.broadcasted_iota(I32, (T, 1), 0)
        s_new = _dot_nt(dqs, newp[:, 0:HD_DSA])

        def count_fn(cand):
            accs = [jnp.where(keys_new >= cand, 1, 0).astype(I32)] + [jnp.zeros((T, LANES), I32)] * 3
            for c in range(NJ):
                ind = jnp.where(keys_ref[c] >= cand, 1, 0).astype(I32)
                for g in range(CH // LANES):
                    accs[g % 4] = accs[g % 4] + ind[:, LANES * g:LANES * g + LANES]
            return jnp.sum((accs[0] + accs[1]) + (accs[2] + accs[3]), axis=1, keepdims=True)
        thr = _kth_largest_key(count_fn, topk, n_valid, kmin, kmax)

        sn = per_head(keys_new >= thr, s_new)
        m = jnp.max(sn, axis=1, keepdims=True)
        for c in range(NJ):
            m = jnp.maximum(m, jnp.max(per_head(keys_ref[c] >= thr, satt_ref[c]), axis=1, keepdims=True))
        pn = jnp.exp2(sn - m)
        l = jnp.sum(pn, axis=1, keepdims=True)
        acc = _dot(pn.astype(BF16), newp[:, HD_DSA:2 * HD_DSA])
        for c in range(NJ):
            p = jnp.exp2(per_head(keys_ref[c] >= thr, satt_ref[c]) - m)
            l = l + jnp.sum(p, axis=1, keepdims=True)
            acc = acc + _dot_nt(p.astype(BF16), rowst_ref[c, HD_DSA:2 * HD_DSA, :])
        o = acc / l
        for hh in range(H_DSA):
            o_ref[:, 64 * hh:64 * hh + 64] = o[T * hh:T * hh + T, :]


def _dsa_sample(l, cache, pt_flat, n_pages, iq_hm, iw, dq_hm, new_rows, T, topk):
    B = pt_flat.shape[0] // n_pages
    NP = min(16, n_pages)
    NJ = n_pages // NP
    CH = NP * PAGE

    def page_spec(p):
        return pl.BlockSpec((None, None, DSA_ROW, PAGE),
                            lambda b, j, pt: (l, pt[b * n_pages + j * NP + p], 0, 0))
    in_specs = [page_spec(p) for p in range(NP)] + [
        pl.BlockSpec((H_IDX, T, D_IDX), lambda b, j, pt: (0, b, 0)),
        pl.BlockSpec((T, LANES), lambda b, j, pt: (b, 0)),
        pl.BlockSpec((H_DSA, T, HD_DSA), lambda b, j, pt: (0, b, 0)),
        pl.BlockSpec((T, DSA_ROW), lambda b, j, pt: (b, 0))]
    grid_spec = pltpu.PrefetchScalarGridSpec(
        num_scalar_prefetch=1, grid=(B, NJ), in_specs=in_specs,
        out_specs=pl.BlockSpec((T, H_DSA * HD_DSA), lambda b, j, pt: (b, 0)),
        scratch_shapes=[pltpu.VMEM((NJ, DSA_ROW, CH), BF16), pltpu.VMEM((PAGE, DSA_ROW), BF16),
                        pltpu.VMEM((NJ, T, CH), I32), pltpu.VMEM((NJ, H_DSA * T, CH), F32),
                        pltpu.VMEM((2, T, LANES), F32)])
    return pl.pallas_call(
        functools.partial(_dsa_sample_kernel, topk, NP, NJ, T),
        grid_spec=grid_spec,
        out_shape=jax.ShapeDtypeStruct((B * T, H_DSA * HD_DSA), F32),
        compiler_params=_cparams(("arbitrary", "arbitrary")),
    )(pt_flat, *([cache] * NP), iq_hm, iw, dq_hm, new_rows)


def _mla_sample_kernel(NP, NJ, T, *refs):
    pages = refs[1:1 + NP]
    qc_ref, new_ref, wuv_ref, o_ref, newp_ref, lat_ref, s_ref, mx_ref = refs[1 + NP:]
    j = pl.program_id(1)
    R = H_MLA * T
    CH = NP * PAGE
    qs = qc_ref[...].reshape(R, MLA_ROW).astype(BF16)

    chunk = jnp.concatenate([pages[p][...] for p in range(NP)], axis=1).astype(BF16)
    lat_ref[j] = chunk[0:D_KV_LAT, :]
    s = _dot(qs, chunk)
    s_ref[j] = s
    smax = functools.reduce(jnp.maximum, [s[:, LANES * g:LANES * g + LANES] for g in range(CH // LANES)])

    @pl.when(j == 0)
    def _():
        mx_ref[...] = smax

    @pl.when(j > 0)
    def _():
        mx_ref[...] = jnp.maximum(mx_ref[...], smax)

    @pl.when(j == NJ - 1)
    def _():
        newp_ref[...] = jnp.zeros((PAGE, MLA_ROW), BF16)
        newp_ref[0:T, :] = new_ref[...].astype(BF16)
        newp = newp_ref[...]
        qi = lax.broadcasted_iota(I32, (R, PAGE), 0) & (T - 1)
        kj = lax.broadcasted_iota(I32, (R, PAGE), 1)
        s_new = jnp.where(kj <= qi, _dot_nt(qs, newp), NEG_BIG)
        m = jnp.max(jnp.maximum(mx_ref[...], s_new), axis=1, keepdims=True)
        p_new = jnp.exp2(s_new - m)
        l = jnp.sum(p_new, axis=1, keepdims=True)
        acc = _dot(p_new.astype(BF16), newp[:, 0:D_KV_LAT])
        for c in range(NJ):
            p = jnp.exp2(s_ref[c] - m)
            l = l + jnp.sum(p, axis=1, keepdims=True)
            acc = acc + _dot_nt(p.astype(BF16), lat_ref[c])
        o_lat = (acc / l).astype(BF16)
        o_all = _dot(o_lat, wuv_ref[...])
        lane_head = lax.broadcasted_iota(I32, (T, H_MLA * DV_MLA), 1) // DV_MLA
        out = jnp.zeros((T, H_MLA * DV_MLA), F32)
        for hh in range(H_MLA):
            out = out + jnp.where(lane_head == hh, o_all[T * hh:T * hh + T, :], 0.0)
        o_ref[...] = out


def _mla_sample(l, cache, pt_flat, n_pages, qc_hm, new_rows, wuv_cat, T):
    B = pt_flat.shape[0] // n_pages
    NP = min(16, n_pages)
    NJ = n_pages // NP
    R = H_MLA * T

    def page_spec(p):
        return pl.BlockSpec((None, None, MLA_ROW, PAGE),
                            lambda b, j, pt: (l, pt[b * n_pages + j * NP + p], 0, 0))
    in_specs = [page_spec(p) for p in range(NP)] + [
        pl.BlockSpec((H_MLA, T, MLA_ROW), lambda b, j, pt: (0, b, 0)),
        pl.BlockSpec((T, MLA_ROW), lambda b, j, pt: (b, 0)),
        pl.BlockSpec((D_KV_LAT, H_MLA * DV_MLA), lambda b, j, pt: (0, 0))]
    grid_spec = pltpu.PrefetchScalarGridSpec(
        num_scalar_prefetch=1, grid=(B, NJ), in_specs=in_specs,
        out_specs=pl.BlockSpec((T, H_MLA * DV_MLA), lambda b, j, pt: (b, 0)),
        scratch_shapes=[pltpu.VMEM((PAGE, MLA_ROW), BF16), pltpu.VMEM((NJ, D_KV_LAT, NP * PAGE), BF16),
                        pltpu.VMEM((NJ, R, NP * PAGE), F32), pltpu.VMEM((R, LANES), F32)])
    return pl.pallas_call(
        functools.partial(_mla_sample_kernel, NP, NJ, T),
        grid_spec=grid_spec,
        out_shape=jax.ShapeDtypeStruct((B * T, H_MLA * DV_MLA), F32),
        compiler_params=_cparams(("arbitrary", "arbitrary")),
    )(pt_flat, *([cache] * NP), qc_hm, new_rows, wuv_cat)


def _mix_kernel(oret_ref, odsa_ref, omla_ref, x_ref, g1_ref, sh2_ref, sc2_ref, n2_ref,
                wout_ref, wrh_ref, wrl_ref, br_ref, x1_ref, h2_ref, gate_ref):
    y = (_dot(oret_ref[...].astype(BF16), wout_ref[0:384, :])
         + _dot(odsa_ref[...].astype(BF16), wout_ref[384:640, :])
         + _dot(omla_ref[...].astype(BF16), wout_ref[640:1024, :]))
    x1 = x_ref[...] + g1_ref[...] * y
    x1_ref[...] = x1
    rstd = lax.rsqrt(jnp.mean(x1 * x1, axis=-1, keepdims=True) + EPS)
    h2 = (x1 * rstd) * n2_ref[...] * (1.0 + sc2_ref[...]) + sh2_ref[...]
    h2b = h2.astype(BF16)
    h2_ref[...] = h2b
    logits = _dot(h2b, wrh_ref[...]) + _dot(h2b, wrl_ref[...]) + br_ref[...]
    lane = lax.broadcasted_iota(I32, logits.shape, 1)
    is_g = (lane >= N_EXPERTS) & (lane < N_EXPERTS + N_GROUPS)
    gl = jnp.where(is_g, logits, -jnp.inf)
    gmax = jnp.max(gl, axis=-1, keepdims=True)
    grp = jnp.min(jnp.where(gl == gmax, lane - N_EXPERTS, N_GROUPS), axis=-1, keepdims=True)
    g_prob = 1.0 / jnp.sum(jnp.exp(gl - gmax), axis=-1, keepdims=True)
    in_grp = (lane < N_EXPERTS) & ((lane >> 3) == grp)
    el = jnp.where(in_grp, logits, -jnp.inf)
    m1 = jnp.max(el, axis=-1, keepdims=True)
    i1 = jnp.min(jnp.where(el == m1, lane, LANES), axis=-1, keepdims=True)
    el2 = jnp.where(lane == i1, -jnp.inf, el)
    m2 = jnp.max(el2, axis=-1, keepdims=True)
    i2 = jnp.min(jnp.where(el2 == m2, lane, LANES), axis=-1, keepdims=True)
    e2 = jnp.exp(m2 - m1)
    den = 1.0 + e2
    gate_ref[...] = jnp.where(lane == i1, 1.0 / den, jnp.where(lane == i2, e2 / den, 0.0)) * g_prob


def _mix(o_ret, o_dsa, o_mla, x, gate1, shift2, scale2, n2, lw, tm):
    T = x.shape[0]
    per_tok = gate1.shape[0] != 1

    def row(width):
        return pl.BlockSpec((tm, width), lambda i: (i, 0))

    def const(shape):
        return pl.BlockSpec(shape, lambda i: (0, 0))
    mod_spec = row(D_MODEL) if per_tok else const((1, D_MODEL))
    return pl.pallas_call(
        _mix_kernel, grid=(T // tm,),
        in_specs=[row(384), row(256), row(384), row(D_MODEL), mod_spec, mod_spec, mod_spec,
                  const((1, D_MODEL)), const((D_MODEL, D_MODEL)), const((D_MODEL, LANES)),
                  const((D_MODEL, LANES)), const((1, LANES))],
        out_specs=[row(D_MODEL), row(D_MODEL), row(LANES)],
        out_shape=[jax.ShapeDtypeStruct((T, D_MODEL), F32), jax.ShapeDtypeStruct((T, D_MODEL), BF16),
                   jax.ShapeDtypeStruct((T, LANES), F32)],
        compiler_params=_cparams(("arbitrary",)),
    )(o_ret, o_dsa, o_mla, x, gate1, shift2, scale2, n2, lw["w_out"], lw["w_rt_hi"], lw["w_rt_lo"], lw["b_rt"])


def _moe_kernel(h_ref, gate_ref, x_ref, g2_ref, w13_ref, w2_ref, o_ref, acc_ref):
    e = pl.program_id(1)

    @pl.when(e == 0)
    def _():
        acc_ref[...] = jnp.zeros_like(acc_ref)

    gate = gate_ref[...]
    lane = lax.broadcasted_iota(I32, gate.shape, 1)
    g_col = jnp.sum(jnp.where(lane == e, gate, 0.0), axis=-1, keepdims=True)
    au = _dot(h_ref[...], w13_ref[0])
    a, u = au[:, 0:D_EXPERT], au[:, D_EXPERT:]
    act = (a * _sigmoid(a)) * u * g_col
    acc_ref[...] += _dot(act.astype(BF16), w2_ref[0])

    @pl.when(e == pl.num_programs(1) - 1)
    def _():
        o_ref[...] = x_ref[...] + g2_ref[...] * acc_ref[...]


def _moe(h2, gate, x1, gate2, w13, w2, tm):
    T = h2.shape[0]
    per_tok = gate2.shape[0] != 1
    mod_spec = (pl.BlockSpec((tm, D_MODEL), lambda i, e: (i, 0)) if per_tok
                else pl.BlockSpec((1, D_MODEL), lambda i, e: (0, 0)))
    return pl.pallas_call(
        _moe_kernel, grid=(T // tm, N_EXPERTS),
        in_specs=[pl.BlockSpec((tm, D_MODEL), lambda i, e: (i, 0)),
                  pl.BlockSpec((tm, LANES), lambda i, e: (i, 0)),
                  pl.BlockSpec((tm, D_MODEL), lambda i, e: (i, 0)),
                  mod_spec,
                  pl.BlockSpec((1, D_MODEL, 2 * D_EXPERT), lambda i, e: (e, 0, 0)),
                  pl.BlockSpec((1, D_EXPERT, D_MODEL), lambda i, e: (e, 0, 0))],
        out_specs=pl.BlockSpec((tm, D_MODEL), lambda i, e: (i, 0)),
        out_shape=jax.ShapeDtypeStruct((T, D_MODEL), F32),
        scratch_shapes=[pltpu.VMEM((tm, D_MODEL), F32)],
        compiler_params=_cparams(("arbitrary", "arbitrary")),
    )(h2, gate, x1, gate2, w13, w2)


def _rope_tables(pos, d):
    half = d // 2
    inv = ROPE_THETA ** (-jnp.arange(half, dtype=F32) * 2.0 / d)
    ang = pos.astype(F32)[:, None] * inv[None, :]
    cos, sin = jnp.cos(ang), jnp.sin(ang)
    reps = LANES // d
    return (jnp.tile(jnp.concatenate([cos, cos], axis=1), (1, reps)),
            jnp.tile(jnp.concatenate([-sin, sin], axis=1), (1, reps)))


def _pad_cols(w, width):
    return jnp.pad(w, ((0, 0), (0, width - w.shape[1])))


def _layer_weights(l, w_in, ret_gn_g, dsa_qn_g, dsa_kn_g, idx_kn_g, mla_cqn_g, mla_ckvn_g, mla_krn_g,
                   mla_qn_g, w_uq, w_uk, w_uv, w_out, w_group, b_group, w_router, b_router, w1, w3, w2):
    offs = np.cumsum((0,) + IN_SPLITS)
    cols = [w_in[l][:, offs[k]:offs[k + 1]] for k in range(len(IN_SPLITS))]
    rq, rk, rv, rg, dq, dk, dv, iq, ik, iw, cq, ckv, kr = cols
    w_pack = jnp.concatenate([rq, rk, rv, rg, dq, dk, dv, iq, _pad_cols(ik, 128), cq, ckv,
                              _pad_cols(kr, 128), _pad_cols(iw, 128)], axis=1).astype(BF16)
    uq = w_uq[l].reshape(D_Q_LAT, H_MLA, D_QK_MLA)
    uq_pack = jnp.concatenate([uq[:, :, :D_NOPE].reshape(D_Q_LAT, 384), uq[:, :, D_NOPE:].reshape(D_Q_LAT, 192),
                               jnp.zeros((D_Q_LAT, 64), F32)], axis=1).astype(BF16)
    ukbd = jnp.zeros((384, 768), F32)
    for hh in range(H_MLA):
        ukbd = ukbd.at[64 * hh:64 * hh + 64, 128 * hh:128 * hh + 128].set(w_uk[l, hh])
    g_mq = jnp.concatenate([jnp.tile(mla_qn_g[l, :D_NOPE], H_MLA), jnp.tile(mla_qn_g[l, D_NOPE:], H_MLA),
                            jnp.zeros((64,), F32)])[None, :]
    head_of = np.concatenate([np.repeat(np.arange(H_MLA), D_NOPE), np.repeat(np.arange(H_MLA), D_ROPE),
                              np.full((64,), -1)])
    bq = jnp.asarray((head_of[:, None] == head_of[None, :]) & (head_of[:, None] >= 0), BF16)
    h64 = np.arange(256) // 64
    b64 = jnp.asarray(h64[:, None] == h64[None, :], BF16)
    w_rt = jnp.concatenate([w_router[l], w_group[l], jnp.zeros((D_MODEL, LANES - N_EXPERTS - N_GROUPS), F32)], axis=1)
    rt_hi = w_rt.astype(BF16)
    rt_lo = (w_rt - rt_hi.astype(F32)).astype(BF16)
    b_rt = jnp.concatenate([b_router[l], b_group[l], jnp.zeros((LANES - N_EXPERTS - N_GROUPS,), F32)])[None, :]
    return dict(
        w_pack=w_pack, w_uq=uq_pack, w_ukbd=ukbd.astype(BF16), b64=b64, bq=bq,
        g_dq=jnp.tile(dsa_qn_g[l], H_DSA)[None, :],
        g_kv=_pad_cols(dsa_kn_g[l][None, :], 128), g_ik=_pad_cols(idx_kn_g[l][None, :], 128),
        g_cq=mla_cqn_g[l][None, :], g_ckv=mla_ckvn_g[l][None, :], g_kr=_pad_cols(mla_krn_g[l][None, :], 128),
        g_mq=g_mq, gn=ret_gn_g[l].reshape(1, H_RET * DV_RET),
        wuvt=jnp.swapaxes(w_uv[l], 1, 2).astype(BF16),
        wuv_cat=jnp.transpose(w_uv[l], (1, 0, 2)).reshape(D_KV_LAT, H_MLA * DV_MLA).astype(BF16),
        w_out=w_out[l].astype(BF16), w_rt_hi=rt_hi, w_rt_lo=rt_lo, b_rt=b_rt,
        w13=jnp.concatenate([w1[l], w3[l]], axis=-1).astype(BF16), w2=w2[l].astype(BF16))


def _pick_tm(T, pref):
    tm = min(pref, T)
    while T % tm:
        tm //= 2
    return tm


def kernel(x_prompt, x_sample, cache_dsa, cache_mla, state_ret, page_table, c_prompt, c_sample, norm1_g, norm2_g, w_ada, b_ada, w_in, ret_gn_g, dsa_qn_g, dsa_kn_g, idx_kn_g, mla_cqn_g, mla_ckvn_g, mla_krn_g, mla_qn_g, w_uq, w_uk, w_uv, w_out, w_group, b_group, w_router, b_router, w1, w3, w2):
    depth = w_in.shape[0]
    BP, S, _ = x_prompt.shape
    B, T, _ = x_sample.shape
    assert BP == 1 and S % 256 == 0 and T == 8
    n_pages = page_table.shape[1]
    past = n_pages * PAGE
    topk_p = min(TOPK_MAX, S // 4)
    topk_s = min(TOPK_MAX, (past + T) // 4)
    log_gammas = tuple(float(np.log(np.float32(1.0) - np.float32(2.0) ** np.float32(-5.0 - h))) for h in range(H_RET))
    KC = _pick_tm(S, 512)
    assert S % (KC * UNROLL) == 0 and S < ZERO_ATOM and past + PAGE < ZERO_ATOM
    cache_dsa_t = jnp.swapaxes(cache_dsa, 2, 3)
    cache_mla_t = jnp.swapaxes(cache_mla, 2, 3)

    n_rows = 1 + B
    r_pad = (-n_rows) % 8
    c_all = jnp.concatenate([c_prompt, c_sample, jnp.zeros((r_pad, D_MODEL), F32)], axis=0)
    mods = _ada(c_all, w_ada, b_ada)

    tabs_p = _rope_tables(jnp.arange(S), 64) + _rope_tables(jnp.arange(S), 32)
    pos_s = jnp.tile(past + jnp.arange(T), B)
    tabs_s = _rope_tables(pos_s, 64) + _rope_tables(pos_s, 32)
    pt_flat = page_table.reshape(-1)

    xp = x_prompt.reshape(S, D_MODEL)
    xs = x_sample.reshape(B * T, D_MODEL)
    tm_p = KC
    tm_s = _pick_tm(B * T, 256)
    zero_state = jnp.zeros((1, H_RET, DK_RET, DV_RET), F32)
    outs = [[] for _ in range(6)]
    for l in range(depth):
        lw = _layer_weights(l, w_in, ret_gn_g, dsa_qn_g, dsa_kn_g, idx_kn_g, mla_cqn_g, mla_ckvn_g, mla_krn_g,
                            mla_qn_g, w_uq, w_uk, w_uv, w_out, w_group, b_group, w_router, b_router, w1, w3, w2)
        mp = [mods[l, 0:1, k * D_MODEL:(k + 1) * D_MODEL] for k in range(6)]
        ms = [jnp.repeat(mods[l, 1:1 + B, k * D_MODEL:(k + 1) * D_MODEL], T, axis=0) for k in range(6)]
        n1 = norm1_g[l][None, :]
        n2 = norm2_g[l][None, :]

        (rq, rk, rv, rg, dq_hm, iq_hm, _, iwt, drow, kd, ikd, kvt, qc_hm, mrow, kvm, ckvt) = _proj(
            xp, mp[0], mp[1], n1, tabs_p, lw, BF16, tm_p)
        o_ret, st_p = _retention(rq, rk, rv, rg, lw["gn"], zero_state, 128, log_gammas, BF16)
        o_dsa = _dsa_prompt(iq_hm, iwt, dq_hm, ikd, kd, kvt, topk_p, KC)
        o_mla = _mla_prompt(qc_hm, kvm, ckvt, lw["wuvt"], KC)
        x1, h2, gate = _mix(o_ret, o_dsa, o_mla, xp, mp[2], mp[3], mp[4], n2, lw, tm_p)
        xp = _moe(h2, gate, x1, mp[5], lw["w13"], lw["w2"], _pick_tm(S, 1024))
        outs[0].append(drow.reshape(1, S, DSA_ROW))
        outs[2].append(mrow.reshape(1, S, MLA_ROW))
        outs[4].append(st_p)

        (rq, rk, rv, rg, dq_hm, iq_hm, iw, _, drow, _, _, _, qc_hm, mrow, _, _) = _proj(
            xs, ms[0], ms[1], n1, tabs_s, lw, F32, tm_s)
        o_ret, st_s = _retention(rq, rk, rv, rg, lw["gn"], state_ret[l].astype(F32), T, log_gammas, F32)
        o_dsa = _dsa_sample(l, cache_dsa_t, pt_flat, n_pages, iq_hm, iw, dq_hm, drow, T, topk_s)
        o_mla = _mla_sample(l, cache_mla_t, pt_flat, n_pages, qc_hm, mrow, lw["wuv_cat"], T)
        x1, h2, gate = _mix(o_ret, o_dsa, o_mla, xs, ms[2], ms[3], ms[4], n2, lw, tm_s)
        xs = _moe(h2, gate, x1, ms[5], lw["w13"], lw["w2"], _pick_tm(B * T, 1024))
        outs[1].append(drow.reshape(B, T, DSA_ROW))
        outs[3].append(mrow.reshape(B, T, MLA_ROW))
        outs[5].append(st_s)

    return (xp.reshape(1, S, D_MODEL), xs.reshape(B, T, D_MODEL),
            jnp.stack(outs[0]), jnp.stack(outs[1]), jnp.stack(outs[2]), jnp.stack(outs[3]),
            jnp.stack(outs[4]), jnp.stack(outs[5]))
```

```python
import functools
import math

import numpy as np
import jax
import jax.numpy as jnp
from jax import lax
from jax.experimental import pallas as pl
from jax.experimental.pallas import tpu as pltpu

F32 = jnp.float32
BF16 = jnp.bfloat16
I32 = jnp.int32

D_MODEL = 1024
PAGE = 128
H_RET, DK_RET, DV_RET = 6, 64, 64
H_DSA, HD_DSA = 4, 64
H_IDX, D_IDX = 8, 64
TOPK_MAX = 256
H_MLA, D_Q_LAT, D_KV_LAT, D_NOPE, D_ROPE, DV_MLA = 6, 256, 128, 64, 32, 64
N_GROUPS, EPG, N_EXPERTS, D_EXPERT = 4, 8, 32, 256
Q_BLOCK = 128
ROPE_THETA = 10000.0
EPS = 1e-6
DSA_ROW = 2 * HD_DSA + D_IDX
MLA_ROW = D_KV_LAT + D_ROPE
D_QK_MLA = D_NOPE + D_ROPE
IN_SPLITS = (384, 384, 384, 384, 256, 64, 64, 512, 64, 8, 256, 128, 32)

C_RQ, C_RK, C_RV, C_RG = 0, 384, 768, 1152
C_DQ, C_KV, C_IQ, C_IK, C_CQ, C_CKV, C_KR, C_IW = 1536, 1792, 1920, 2432, 2560, 2816, 2944, 3072
D_PACK = 3200
MLA_QW = 640

LANES = 128
VMEM_LIMIT = 56 * 1024 * 1024

INT_MIN = -2 ** 31
NEG_BIG = -1e30
LOG2E = math.log2(math.e)
CKVT_ROWS = D_KV_LAT + 16
UNROLL = 2
DSA_PAGES_PER_STEP = 16
MLA_PAGES_PER_STEP = 64
ZERO_ATOM = 1 << 15
KEY_NEG_INF = (0xFF800000 ^ 0x7FFFFFFF) - 2 ** 32


def _cparams(sem):
    return pltpu.CompilerParams(dimension_semantics=sem, vmem_limit_bytes=VMEM_LIMIT)


def _dot(a, b):
    return jnp.dot(a, b, preferred_element_type=F32)


def _dot_nt(a, b):
    return lax.dot_general(a, b, (((1,), (1,)), ((), ())), preferred_element_type=F32)


def _split(x):
    hi = x.astype(BF16)
    lo = (x - hi.astype(F32)).astype(BF16)
    return hi, lo


def _dot_f32_lhs(x, b_bf16):
    hi, lo = _split(x)
    return _dot(hi, b_bf16) + _dot(lo, b_bf16)


def _sigmoid(x):
    return 1.0 / (1.0 + jnp.exp(-x))


def _sort_key(x):
    b = lax.bitcast_convert_type(x, I32)
    return b ^ ((b >> 31) & 0x7FFFFFFF)


def _rope(x, cos, sin, half):
    w = x.shape[-1]
    fwd = pltpu.roll(x, w - half, axis=1)
    bwd = pltpu.roll(x, half, axis=1)
    lane = lax.broadcasted_iota(I32, x.shape, 1)
    first = (lane & (2 * half - 1)) < half
    return x * cos + jnp.where(first, fwd, bwd) * sin


def _tile_lanes(t, n):
    return t if n == 1 else jnp.concatenate([t] * n, axis=1)


def _ada_kernel(c_ref, w_ref, b_ref, o_ref):
    c = c_ref[...]
    a = c * _sigmoid(c)
    ah, al = _split(a)
    w = w_ref[0]
    wh, wl = _split(w)
    o_ref[0] = _dot(ah, wh) + _dot(ah, wl) + _dot(al, wh) + b_ref[0]


def _ada(c_all, w_ada, b_ada):
    L = w_ada.shape[0]
    R = c_all.shape[0]
    tn = 1024
    return pl.pallas_call(
        _ada_kernel,
        grid=(L, 6 * D_MODEL // tn),
        in_specs=[pl.BlockSpec((R, D_MODEL), lambda l, j: (0, 0)),
                  pl.BlockSpec((1, D_MODEL, tn), lambda l, j: (l, 0, j)),
                  pl.BlockSpec((1, 1, tn), lambda l, j: (l, 0, j))],
        out_specs=pl.BlockSpec((1, R, tn), lambda l, j: (l, 0, j)),
        out_shape=jax.ShapeDtypeStruct((L, R, 6 * D_MODEL), F32),
        compiler_params=_cparams(("arbitrary", "arbitrary")),
    )(c_all, w_ada, b_ada.reshape(L, 1, 6 * D_MODEL))


def _proj_kernel(x_ref, sh_ref, sc_ref, g1_ref, c64_ref, s64_ref, c32_ref, s32_ref,
                 w_ref, gq_ref, gkv_ref, gik_ref, gcq_ref, gckv_ref, gkr_ref, gmq_ref,
                 wuq_ref, wuk_ref, b64_ref, bq_ref,
                 rq_ref, rk_ref, rv_ref, rg_ref, dq_ref, iq_ref, iw_ref, iwt_ref,
                 drow_ref, kd_ref, ikd_ref, kvt_ref, qc_ref, mrow_ref, kvm_ref, ckvt_ref):
    x = x_ref[...]
    rstd = lax.rsqrt(jnp.mean(x * x, axis=-1, keepdims=True) + EPS)
    h = (x * rstd) * g1_ref[...] * (1.0 + sc_ref[...]) + sh_ref[...]
    hb = h.astype(BF16)
    c64, s64, c32, s32 = c64_ref[...], s64_ref[...], c32_ref[...], s32_ref[...]

    def zcols(c0, width):
        return _dot(hb, w_ref[:, c0:c0 + width])

    rq_ref[...] = _rope(zcols(C_RQ, 384), _tile_lanes(c64, 3), _tile_lanes(s64, 3), 32)
    rk_ref[...] = _rope(zcols(C_RK, 384), _tile_lanes(c64, 3), _tile_lanes(s64, 3), 32) * (DK_RET ** -0.5)
    rv_ref[...] = zcols(C_RV, 384)
    rg_ref[...] = zcols(C_RG, 384)

    dq = zcols(C_DQ, 256)
    ss = _dot_f32_lhs(dq * dq, b64_ref[...])
    dq = dq * lax.rsqrt(ss * (1.0 / HD_DSA) + EPS) * gq_ref[...]
    dq = _rope(dq, _tile_lanes(c64, 2), _tile_lanes(s64, 2), 32) * (HD_DSA ** -0.5 * LOG2E)
    for hh in range(H_DSA):
        dq_ref[hh] = dq[:, 64 * hh:64 * hh + 64].astype(dq_ref.dtype)

    kv = zcols(C_KV, 128)
    lane = lax.broadcasted_iota(I32, kv.shape, 1)
    is_k = lane < HD_DSA
    ssk = jnp.sum(jnp.where(is_k, kv * kv, 0.0), axis=-1, keepdims=True)
    kn = kv * lax.rsqrt(ssk * (1.0 / HD_DSA) + EPS) * gkv_ref[...]
    kv = jnp.where(is_k, _rope(kn, c64, s64, 32), kv)
    drow_ref[:, 0:128] = kv
    kvb = kv.astype(BF16)
    kd_ref[...] = kvb[:, 0:64]
    kvt_ref[0] = jnp.where(is_k, 1.0, kv).T.astype(BF16)

    iq = _rope(zcols(C_IQ, 512), _tile_lanes(c64, 4), _tile_lanes(s64, 4), 32) * (D_IDX ** -0.5)
    for hh in range(H_IDX):
        iq_ref[hh] = iq[:, 64 * hh:64 * hh + 64].astype(iq_ref.dtype)
    ik = zcols(C_IK, 128)
    ssi = jnp.sum(ik * ik, axis=-1, keepdims=True)
    ik = _rope(ik * lax.rsqrt(ssi * (1.0 / D_IDX) + EPS) * gik_ref[...], c64, s64, 32)
    drow_ref[:, 128:192] = ik[:, 0:64]
    ikd_ref[...] = ik[:, 0:64].astype(BF16)
    iw = zcols(C_IW, 128) * (H_IDX ** -0.5)
    iw_ref[...] = iw
    iwt_ref[...] = iw.T[0:H_IDX, :]

    cq = zcols(C_CQ, 256)
    cq = cq * lax.rsqrt(jnp.mean(cq * cq, axis=-1, keepdims=True) + EPS) * gcq_ref[...]
    q = _dot(cq.astype(BF16), wuq_ref[...])
    ssq = _dot_f32_lhs(q * q, bq_ref[...])
    q = q * lax.rsqrt(ssq * (1.0 / D_QK_MLA) + EPS) * gmq_ref[...]
    scale = D_QK_MLA ** -0.5 * LOG2E
    q_lat = _dot(q[:, 0:384].astype(BF16), wuk_ref[...]) * scale
    q_rope = _rope(q[:, 384:640], _tile_lanes(c32, 2), _tile_lanes(s32, 2), 16) * scale
    for hh in range(H_MLA):
        qc_ref[hh, :, 0:128] = q_lat[:, 128 * hh:128 * hh + 128].astype(qc_ref.dtype)
        qc_ref[hh, :, 128:160] = q_rope[:, 32 * hh:32 * hh + 32].astype(qc_ref.dtype)
    ckv = zcols(C_CKV, 128)
    ckv = ckv * lax.rsqrt(jnp.mean(ckv * ckv, axis=-1, keepdims=True) + EPS) * gckv_ref[...]
    kr = zcols(C_KR, 128)
    ssr = jnp.sum(kr * kr, axis=-1, keepdims=True)
    kr = _rope(kr * lax.rsqrt(ssr * (1.0 / D_ROPE) + EPS) * gkr_ref[...], c32, s32, 16)
    mrow_ref[:, 0:128] = ckv
    mrow_ref[:, 128:160] = kr[:, 0:32]
    kvm_ref[:, 0:128] = ckv.astype(BF16)
    kvm_ref[:, 128:160] = kr[:, 0:32].astype(BF16)
    ckvt_ref[0, 0:D_KV_LAT, :] = ckv.T.astype(BF16)
    ckvt_ref[0, D_KV_LAT:CKVT_ROWS, :] = jnp.ones((CKVT_ROWS - D_KV_LAT, ckv.shape[0]), BF16)


def _proj(x, shift, scale, g1, tabs, lw, qdtype, tm):
    T = x.shape[0]
    nb = T // tm
    per_tok = shift.shape[0] != 1

    def row(width):
        return pl.BlockSpec((tm, width), lambda i: (i, 0))

    def const(shape):
        nd = len(shape)
        return pl.BlockSpec(shape, lambda i: (0,) * nd)

    mod_spec = row(D_MODEL) if per_tok else const((1, D_MODEL))
    in_specs = [row(D_MODEL), mod_spec, mod_spec, const((1, D_MODEL)),
                row(LANES), row(LANES), row(LANES), row(LANES),
                const((D_MODEL, D_PACK)),
                const((1, 256)), const((1, 128)), const((1, 128)), const((1, 256)),
                const((1, 128)), const((1, 128)), const((1, MLA_QW)),
                const((D_Q_LAT, MLA_QW)), const((384, 768)), const((256, 256)), const((MLA_QW, MLA_QW))]
    out_shape = [
        jax.ShapeDtypeStruct((T, 384), F32), jax.ShapeDtypeStruct((T, 384), F32),
        jax.ShapeDtypeStruct((T, 384), F32), jax.ShapeDtypeStruct((T, 384), F32),
        jax.ShapeDtypeStruct((H_DSA, T, 64), qdtype), jax.ShapeDtypeStruct((H_IDX, T, 64), qdtype),
        jax.ShapeDtypeStruct((T, LANES), F32), jax.ShapeDtypeStruct((H_IDX, T), F32),
        jax.ShapeDtypeStruct((T, DSA_ROW), F32), jax.ShapeDtypeStruct((T, 64), BF16),
        jax.ShapeDtypeStruct((T, 64), BF16), jax.ShapeDtypeStruct((nb, 128, tm), BF16),
        jax.ShapeDtypeStruct((H_MLA, T, 160), qdtype), jax.ShapeDtypeStruct((T, MLA_ROW), F32),
        jax.ShapeDtypeStruct((T, MLA_ROW), BF16), jax.ShapeDtypeStruct((nb, CKVT_ROWS, tm), BF16)]
    out_specs = [
        row(384), row(384), row(384), row(384),
        pl.BlockSpec((H_DSA, tm, 64), lambda i: (0, i, 0)), pl.BlockSpec((H_IDX, tm, 64), lambda i: (0, i, 0)),
        row(LANES), pl.BlockSpec((H_IDX, tm), lambda i: (0, i)),
        row(DSA_ROW), row(64), row(64), pl.BlockSpec((1, 128, tm), lambda i: (i, 0, 0)),
        pl.BlockSpec((H_MLA, tm, 160), lambda i: (0, i, 0)), row(MLA_ROW), row(MLA_ROW),
        pl.BlockSpec((1, CKVT_ROWS, tm), lambda i: (i, 0, 0))]
    return pl.pallas_call(
        _proj_kernel, grid=(nb,), in_specs=in_specs, out_specs=out_specs, out_shape=out_shape,
        compiler_params=_cparams(("arbitrary",)),
    )(x, shift, scale, g1, *tabs, lw["w_pack"], lw["g_dq"], lw["g_kv"], lw["g_ik"], lw["g_cq"],
      lw["g_ckv"], lw["g_kr"], lw["g_mq"], lw["w_uq"], lw["w_ukbd"], lw["b64"], lw["bq"])


def _ret_kernel(log_gammas, C, q_ref, k_ref, v_ref, g_ref, gn_ref, s0_ref, o_ref, st_ref):
    first = pl.program_id(1) == 0

    @pl.when(first)
    def _():
        st_ref[...] = s0_ref[...]

    ri = lax.broadcasted_iota(I32, (C, C), 0)
    ci = lax.broadcasted_iota(I32, (C, C), 1)
    rel = (ri - ci).astype(F32)
    pos = lax.broadcasted_iota(I32, (C, 1), 0).astype(F32)
    q_all, k_all, v_all, g_all, gn = q_ref[...], k_ref[...], v_ref[...], g_ref[...], gn_ref[...]
    for hh in range(H_RET):
        lg = log_gammas[hh]
        sl = slice(64 * hh, 64 * hh + 64)
        q, k, v = q_all[:, sl], k_all[:, sl], v_all[:, sl]
        qb, vb = q.astype(BF16), v.astype(BF16)
        decay = jnp.where(rel >= 0, jnp.exp(lg * jnp.maximum(rel, 0.0)), 0.0)
        attn = _dot_nt(qb, k.astype(BF16)) * decay
        state = st_ref[0, hh]
        o = _dot(attn.astype(BF16), vb) + _dot(qb, state.astype(BF16)) * jnp.exp(lg * (pos + 1.0))
        k_dec = k * jnp.exp(lg * (C - 1.0 - pos))
        st_ref[0, hh] = state * math.exp(lg * C) + _dot(k_dec.T.astype(BF16), vb)
        mu = jnp.mean(o, axis=-1, keepdims=True)
        d = o - mu
        var = jnp.mean(d * d, axis=-1, keepdims=True)
        gate = g_all[:, sl]
        o_ref[:, sl] = (d * lax.rsqrt(var + EPS) * gn[:, sl] * (gate * _sigmoid(gate))).astype(o_ref.dtype)


def _retention(rq, rk, rv, rg, gn, state0, C, log_gammas, out_dtype):
    NB = state0.shape[0]
    T = rq.shape[0]
    nc = T // (NB * C)
    row = pl.BlockSpec((C, 384), lambda b, c: (b * nc + c, 0))
    st = pl.BlockSpec((1, H_RET, DK_RET, DV_RET), lambda b, c: (b, 0, 0, 0))
    return pl.pallas_call(
        functools.partial(_ret_kernel, log_gammas, C),
        grid=(NB, nc),
        in_specs=[row, row, row, row, pl.BlockSpec((1, 384), lambda b, c: (0, 0)), st],
        out_specs=[row, st],
        out_shape=[jax.ShapeDtypeStruct((T, 384), out_dtype),
                   jax.ShapeDtypeStruct((NB, H_RET, DK_RET, DV_RET), F32)],
        compiler_params=_cparams(("arbitrary", "arbitrary")),
    )(rq, rk, rv, rg, gn, state0)


def _count_ge(keys_ref, nch, kc, cand):
    def body(c, acc):
        blk = keys_ref[pl.ds(pl.multiple_of(c * kc, kc), kc), :]
        ind = jnp.where(blk >= cand, 1, 0).astype(I32)
        return acc + jnp.sum(ind.reshape(kc // 8, 8, LANES), axis=0)
    acc = lax.fori_loop(0, nch, body, jnp.zeros((8, LANES), I32))
    return jnp.sum(acc, axis=0, keepdims=True)


def _index_key(score, pos):
    k = _sort_key(score)
    return jnp.where(k > 0, k + ZERO_ATOM, jnp.where(k >= -1, ZERO_ATOM - pos, k))


def _index_key_of_float(x):
    k = _sort_key(x)
    return jnp.where(k > 0, k + ZERO_ATOM, k)


def _index_key_to_float(k):
    k = jnp.where(k > ZERO_ATOM, k - ZERO_ATOM, jnp.where(k > 0, 0, k))
    return lax.bitcast_convert_type(k ^ ((k >> 31) & 0x7FFFFFFF), F32)


def _index_key_range(fmin, fmax):
    kmin, kmax = _sort_key(fmin), _sort_key(fmax)
    return (jnp.where(kmin > 0, kmin + ZERO_ATOM, jnp.where(kmin >= -1, 1, kmin)),
            jnp.where(kmax > 0, kmax + ZERO_ATOM, jnp.where(kmax >= -1, ZERO_ATOM, kmax)))


def _kth_largest_key(count_fn, topk, n_valid, kmin, kmax):
    tgt = math.log2(topk)

    def lg(c):
        return jnp.log2(jnp.maximum(c.astype(F32), 0.5))

    def active(lo, hi, c_lo):
        return (c_lo > topk) & (hi > lo + 1)

    def cond(st):
        it, lo, hi, c_lo = st[:4]
        return (it < 128) & (jnp.max(jnp.where(active(lo, hi, c_lo), 1, 0)) > 0)

    def body(st):
        it, lo, hi, c_lo, c_hi, l_lo, l_hi, side = st
        f_lo, f_hi = _index_key_to_float(lo), _index_key_to_float(hi)
        by_score = _index_key_of_float(f_lo + (f_hi - f_lo) * ((l_lo - tgt) / (l_lo - l_hi)))
        frac = (c_lo - topk).astype(F32) / jnp.maximum(c_lo - c_hi, 1).astype(F32)
        by_pos = lo + 1 + ((hi - lo).astype(F32) * frac).astype(I32)
        cand = jnp.where((lo >= 1) & (hi <= ZERO_ATOM + 1), by_pos, by_score)
        cand = jnp.where(it == 0, ZERO_ATOM + 1, jnp.where(it == 1, 1, cand))
        mid = (lo >> 1) + (hi >> 1) + (lo & hi & 1)
        cand = jnp.where((cand > lo) & (cand < hi), cand, mid)
        cand = jnp.where((it >= 14) & ((it & 1) == 0), mid, cand)
        c = count_fn(cand)
        act = active(lo, hi, c_lo)
        up = act & (c >= topk)
        dn = act & (c < topk)
        lc = lg(c)
        l_lo_n = jnp.where(up, lc, jnp.where(dn & (side < 0), tgt + 0.5 * (l_lo - tgt), l_lo))
        l_hi_n = jnp.where(dn, lc, jnp.where(up & (side > 0), tgt + 0.5 * (l_hi - tgt), l_hi))
        return (it + 1, jnp.where(up, cand, lo), jnp.where(dn, cand, hi), jnp.where(up, c, c_lo),
                jnp.where(dn, c, c_hi), l_lo_n, l_hi_n, jnp.where(up, 1, jnp.where(dn, -1, side)))
    zero = jnp.zeros(kmin.shape, I32)
    init = (jnp.int32(0), kmin, kmax + 1, n_valid, zero, lg(n_valid), jnp.full(kmin.shape, -1.0, F32), zero)
    return lax.while_loop(cond, body, lax.fori_loop(0, 8, lambda _, st: body(st), init))[1]


def _dsa_prompt_kernel(topk, kc, iq_ref, iwt_ref, dq_ref, ikd_ref, kd_ref, kvt_ref, o_ref, keys_ref):
    i = pl.program_id(0)
    sup = kc * UNROLL
    last = (i * Q_BLOCK) // sup
    iqs = iq_ref[...].reshape(H_IDX * Q_BLOCK, D_IDX)
    iwt = iwt_ref[...]

    def idx_scores(start):
        s = _dot_nt(ikd_ref[pl.ds(start, kc), :], iqs)
        score = jnp.maximum(s[:, 0:LANES], 0.0) * iwt[0:1, :]
        for hh in range(1, H_IDX):
            score = score + jnp.maximum(s[:, LANES * hh:LANES * hh + LANES], 0.0) * iwt[hh:hh + 1, :]
        return score

    def fold(x, op):
        return op(x.reshape(kc // 8, 8, LANES), axis=0)

    def idx_body(c, carry):
        mx, mn = carry
        for u in range(UNROLL):
            start = pl.multiple_of((c * UNROLL + u) * kc, kc)
            sc = idx_scores(start)
            keys_ref[pl.ds(start, kc), :] = _index_key(sc, start + row_iota)
            mx, mn = jnp.maximum(mx, fold(sc, jnp.max)), jnp.minimum(mn, fold(sc, jnp.min))
        return mx, mn
    row_iota = lax.broadcasted_iota(I32, (kc, LANES), 0)
    mx, mn = lax.fori_loop(0, last, idx_body, (jnp.full((8, LANES), -jnp.inf, F32), jnp.full((8, LANES), jnp.inf, F32)))
    q_pos = i * Q_BLOCK + lax.broadcasted_iota(I32, (kc, LANES), 1)
    for u in range(UNROLL):
        start = pl.multiple_of((last * UNROLL + u) * kc, kc)
        valid = start + row_iota <= q_pos
        sc = idx_scores(start)
        sc_lo = jnp.where(valid, sc, -jnp.inf)
        keys_ref[pl.ds(start, kc), :] = _index_key(sc_lo, start + row_iota)
        mx = jnp.maximum(mx, fold(sc_lo, jnp.max))
        mn = jnp.minimum(mn, fold(jnp.where(valid, sc, jnp.inf), jnp.min))

    kmin, kmax = _index_key_range(jnp.min(mn, axis=0, keepdims=True), jnp.max(mx, axis=0, keepdims=True))
    n_valid = i * Q_BLOCK + lax.broadcasted_iota(I32, (1, LANES), 1) + 1
    thr = _kth_largest_key(lambda cand: _count_ge(keys_ref, (last + 1) * UNROLL, kc, cand), topk, n_valid, kmin, kmax)

    dqs = dq_ref[...].reshape(H_DSA * Q_BLOCK, HD_DSA)
    W = H_DSA * LANES

    def att_step(cs, carry):
        m, acc = carry
        start = pl.multiple_of(cs * kc, kc)
        s = _dot_nt(kd_ref[pl.ds(start, kc), :], dqs)
        sel = keys_ref[pl.ds(start, kc), :] >= thr
        s = jnp.concatenate([jnp.where(sel, s[:, LANES * hh:LANES * hh + LANES], NEG_BIG)
                             for hh in range(H_DSA)], axis=1)
        m_new = jnp.maximum(m, jnp.max(s, axis=0, keepdims=True))
        p = jnp.exp2(s - m_new)
        acc = jnp.exp2(m - m_new) * acc + _dot(kvt_ref[cs], p.astype(BF16))
        return m_new, acc

    def att_body(c, carry):
        for u in range(UNROLL):
            carry = att_step(c * UNROLL + u, carry)
        return carry
    m0 = jnp.full((1, W), NEG_BIG, F32)
    m, acc = lax.fori_loop(0, last + 1, att_body, (m0, jnp.zeros((LANES, W), F32)))
    o_t = acc[HD_DSA:, :] / acc[0:1, :]
    o_hq = jnp.concatenate([o_t[:, LANES * hh:LANES * hh + LANES] for hh in range(H_DSA)], axis=0)
    o_ref[...] = o_hq.T.astype(o_ref.dtype)


def _dsa_prompt(iq_hm, iwt, dq_hm, ikd, kd, kvt, topk, kc):
    T = ikd.shape[0]
    nq = T // Q_BLOCK
    nkb = kvt.shape[0]
    return pl.pallas_call(
        functools.partial(_dsa_prompt_kernel, topk, kc),
        grid=(nq,),
        in_specs=[pl.BlockSpec((H_IDX, Q_BLOCK, D_IDX), lambda i: (0, i, 0)),
                  pl.BlockSpec((H_IDX, Q_BLOCK), lambda i: (0, i)),
                  pl.BlockSpec((H_DSA, Q_BLOCK, HD_DSA), lambda i: (0, i, 0)),
                  pl.BlockSpec((T, D_IDX), lambda i: (0, 0)),
                  pl.BlockSpec((T, HD_DSA), lambda i: (0, 0)),
                  pl.BlockSpec((nkb, 128, kc), lambda i: (0, 0, 0))],
        out_specs=pl.BlockSpec((Q_BLOCK, H_DSA * HD_DSA), lambda i: (i, 0)),
        out_shape=jax.ShapeDtypeStruct((T, H_DSA * HD_DSA), BF16),
        scratch_shapes=[pltpu.VMEM((nkb * kc, LANES), I32)],
        compiler_params=_cparams(("arbitrary",)),
    )(iq_hm, iwt, dq_hm, ikd, kd, kvt)


def _mla_prompt_kernel(kc, qc_ref, kvm_ref, ckvt_ref, wuvt_ref, o_ref):
    i = pl.program_id(0)
    last = (i * Q_BLOCK) // (kc * UNROLL)
    W = H_MLA * LANES
    qs = qc_ref[...].reshape(H_MLA * Q_BLOCK, MLA_ROW)

    def step(c, carry, causal):
        m, acc = carry
        start = pl.multiple_of(c * kc, kc)
        s = _dot_nt(kvm_ref[pl.ds(start, kc), :], qs)
        if causal:
            q_pos = i * Q_BLOCK + (lax.broadcasted_iota(I32, (kc, W), 1) & (LANES - 1))
            key_pos = start + lax.broadcasted_iota(I32, (kc, W), 0)
            s = jnp.where(key_pos <= q_pos, s, NEG_BIG)
        m_new = jnp.maximum(m, jnp.max(s, axis=0, keepdims=True))
        p = jnp.exp2(s - m_new)
        acc = jnp.exp2(m - m_new) * acc + _dot(ckvt_ref[c], p.astype(BF16))
        return m_new, acc
    carry = (jnp.full((1, W), NEG_BIG, F32), jnp.zeros((CKVT_ROWS, W), F32))

    def body(c, cr):
        for u in range(UNROLL):
            cr = step(c * UNROLL + u, cr, False)
        return cr
    carry = lax.fori_loop(0, last, body, carry)
    for u in range(UNROLL):
        carry = step(last * UNROLL + u, carry, True)
    m, acc = carry
    o_lat = (acc[0:D_KV_LAT, :] / acc[D_KV_LAT:D_KV_LAT + 1, :]).astype(BF16)
    outs = [_dot(wuvt_ref[hh], o_lat[:, LANES * hh:LANES * hh + LANES]) for hh in range(H_MLA)]
    o_ref[...] = jnp.concatenate(outs, axis=0).T.astype(o_ref.dtype)


def _mla_prompt(qc_hm, kvm, ckvt, wuvt, kc):
    T = kvm.shape[0]
    nq = T // Q_BLOCK
    nkb = ckvt.shape[0]
    return pl.pallas_call(
        functools.partial(_mla_prompt_kernel, kc),
        grid=(nq,),
        in_specs=[pl.BlockSpec((H_MLA, Q_BLOCK, MLA_ROW), lambda i: (0, i, 0)),
                  pl.BlockSpec((T, MLA_ROW), lambda i: (0, 0)),
                  pl.BlockSpec((nkb, CKVT_ROWS, kc), lambda i: (0, 0, 0)),
                  pl.BlockSpec((H_MLA, DV_MLA, D_KV_LAT), lambda i: (0, 0, 0))],
        out_specs=pl.BlockSpec((Q_BLOCK, H_MLA * DV_MLA), lambda i: (i, 0)),
        out_shape=jax.ShapeDtypeStruct((T, H_MLA * DV_MLA), BF16),
        compiler_params=_cparams(("arbitrary",)),
    )(qc_hm, kvm, ckvt, wuvt)


def _dsa_sample_kernel(topk, NP, NJ, T, *refs):
    pages = refs[1:1 + NP]
    iq_ref, iw_ref, dq_ref, new_ref, o_ref, rowst_ref, newp_ref, keys_ref, satt_ref, mm_ref = refs[1 + NP:]
    j = pl.program_id(1)
    CH = NP * PAGE
    iqs = iq_ref[...].reshape(H_IDX * T, D_IDX).astype(BF16)
    dqs = dq_ref[...].reshape(H_DSA * T, HD_DSA).astype(BF16)
    iw = iw_ref[...]

    def head_sum(s):
        score = jnp.maximum(s[0:T, :], 0.0) * iw[:, 0:1]
        for hh in range(1, H_IDX):
            score = score + jnp.maximum(s[T * hh:T * hh + T, :], 0.0) * iw[:, hh:hh + 1]
        return score

    def per_head(sel, s):
        return jnp.concatenate([jnp.where(sel, s[T * hh:T * hh + T, :], NEG_BIG) for hh in range(H_DSA)], axis=0)

    chunk = jnp.concatenate([pages[p][...] for p in range(NP)], axis=1).astype(BF16)
    rowst_ref[j] = chunk
    sc = head_sum(_dot(iqs, chunk[2 * HD_DSA:DSA_ROW, :]))
    keys_ref[j] = _index_key(sc, j * CH + lax.broadcasted_iota(I32, (T, CH), 1))
    satt_ref[j] = _dot(dqs, chunk[0:HD_DSA, :])

    def fold(x, op):
        return functools.reduce(op, [x[:, LANES * g:LANES * g + LANES] for g in range(x.shape[1] // LANES)])

    @pl.when(j == 0)
    def _():
        mm_ref[0] = jnp.full((T, LANES), -jnp.inf, F32)
        mm_ref[1] = jnp.full((T, LANES), jnp.inf, F32)
    mm_ref[0] = jnp.maximum(mm_ref[0], fold(sc, jnp.maximum))
    mm_ref[1] = jnp.minimum(mm_ref[1], fold(sc, jnp.minimum))

    @pl.when(j == NJ - 1)
    def _():
        newp_ref[...] = jnp.zeros((PAGE, DSA_ROW), BF16)
        newp_ref[0:T, :] = new_ref[...].astype(BF16)
        newp = newp_ref[...]
        qi = lax.broadcasted_iota(I32, (T, PAGE), 0)
        kj = lax.broadcasted_iota(I32, (T, PAGE), 1)
        idx_new = head_sum(_dot_nt(iqs, newp[:, 2 * HD_DSA:DSA_ROW]))
        sc_new = jnp.where(kj <= qi, idx_new, -jnp.inf)
        keys_new = _index_key(sc_new, NJ * CH + kj)
        fmax = jnp.max(jnp.maximum(mm_ref[0], sc_new), axis=1, keepdims=True)
        fmin = jnp.min(jnp.minimum(mm_ref[1], jnp.where(kj <= qi, idx_new, jnp.inf)), axis=1, keepdims=True)
        kmin, kmax = _index_key_range(fmin, fmax)
        n_valid = NJ * CH + 1 + lax.broadcasted_iota(I32, (T, 1), 0)
        s_new = _dot_nt(dqs, newp[:, 0:HD_DSA])

        def count_fn(cand):
            accs = [jnp.where(keys_new >= cand, 1, 0).astype(I32)] + [jnp.zeros((T, LANES), I32)] * 3
            for c in range(NJ):
                ind = jnp.where(keys_ref[c] >= cand, 1, 0).astype(I32)
                for g in range(CH // LANES):
                    accs[g % 4] = accs[g % 4] + ind[:, LANES * g:LANES * g + LANES]
            return jnp.sum((accs[0] + accs[1]) + (accs[2] + accs[3]), axis=1, keepdims=True)
        thr = _kth_largest_key(count_fn, topk, n_valid, kmin, kmax)

        sn = per_head(keys_new >= thr, s_new)
        m = jnp.max(sn, axis=1, keepdims=True)
        for c in range(NJ):
            m = jnp.maximum(m, jnp.max(per_head(keys_ref[c] >= thr, satt_ref[c]), axis=1, keepdims=True))
        pn = jnp.exp2(sn - m)
        l = jnp.sum(pn, axis=1, keepdims=True)
        acc = _dot(pn.astype(BF16), newp[:, HD_DSA:2 * HD_DSA])
        for c in range(NJ):
            p = jnp.exp2(per_head(keys_ref[c] >= thr, satt_ref[c]) - m)
            l = l + jnp.sum(p, axis=1, keepdims=True)
            acc = acc + _dot_nt(p.astype(BF16), rowst_ref[c, HD_DSA:2 * HD_DSA, :])
        o = acc / l
        for hh in range(H_DSA):
            o_ref[:, 64 * hh:64 * hh + 64] = o[T * hh:T * hh + T, :]


def _dsa_sample(l, cache, pt_flat, n_pages, iq_hm, iw, dq_hm, new_rows, T, topk):
    B = pt_flat.shape[0] // n_pages
    NP = min(DSA_PAGES_PER_STEP, n_pages)
    NJ = n_pages // NP
    CH = NP * PAGE

    def page_spec(p):
        return pl.BlockSpec((None, None, DSA_ROW, PAGE),
                            lambda b, j, pt: (l, pt[b * n_pages + j * NP + p], 0, 0))
    in_specs = [page_spec(p) for p in range(NP)] + [
        pl.BlockSpec((H_IDX, T, D_IDX), lambda b, j, pt: (0, b, 0)),
        pl.BlockSpec((T, LANES), lambda b, j, pt: (b, 0)),
        pl.BlockSpec((H_DSA, T, HD_DSA), lambda b, j, pt: (0, b, 0)),
        pl.BlockSpec((T, DSA_ROW), lambda b, j, pt: (b, 0))]
    grid_spec = pltpu.PrefetchScalarGridSpec(
        num_scalar_prefetch=1, grid=(B, NJ), in_specs=in_specs,
        out_specs=pl.BlockSpec((T, H_DSA * HD_DSA), lambda b, j, pt: (b, 0)),
        scratch_shapes=[pltpu.VMEM((NJ, DSA_ROW, CH), BF16), pltpu.VMEM((PAGE, DSA_ROW), BF16),
                        pltpu.VMEM((NJ, T, CH), I32), pltpu.VMEM((NJ, H_DSA * T, CH), F32),
                        pltpu.VMEM((2, T, LANES), F32)])
    return pl.pallas_call(
        functools.partial(_dsa_sample_kernel, topk, NP, NJ, T),
        grid_spec=grid_spec,
        out_shape=jax.ShapeDtypeStruct((B * T, H_DSA * HD_DSA), F32),
        compiler_params=_cparams(("arbitrary", "arbitrary")),
    )(pt_flat, *([cache] * NP), iq_hm, iw, dq_hm, new_rows)


def _mla_sample_kernel(NP, NJ, T, *refs):
    pages = refs[1:1 + NP]
    qc_ref, new_ref, wuv_ref, o_ref, newp_ref, lat_ref, s_ref, mx_ref = refs[1 + NP:]
    j = pl.program_id(1)
    R = H_MLA * T
    CH = NP * PAGE
    qs = qc_ref[...].reshape(R, MLA_ROW).astype(BF16)

    chunk = jnp.concatenate([pages[p][...] for p in range(NP)], axis=1).astype(BF16)
    lat_ref[j] = chunk[0:D_KV_LAT, :]
    s = _dot(qs, chunk)
    s_ref[j] = s
    smax = functools.reduce(jnp.maximum, [s[:, LANES * g:LANES * g + LANES] for g in range(CH // LANES)])

    @pl.when(j == 0)
    def _():
        mx_ref[...] = smax

    @pl.when(j > 0)
    def _():
        mx_ref[...] = jnp.maximum(mx_ref[...], smax)

    @pl.when(j == NJ - 1)
    def _():
        newp_ref[...] = jnp.zeros((PAGE, MLA_ROW), BF16)
        newp_ref[0:T, :] = new_ref[...].astype(BF16)
        newp = newp_ref[...]
        qi = lax.broadcasted_iota(I32, (R, PAGE), 0) & (T - 1)
        kj = lax.broadcasted_iota(I32, (R, PAGE), 1)
        s_new = jnp.where(kj <= qi, _dot_nt(qs, newp), NEG_BIG)
        m = jnp.max(jnp.maximum(mx_ref[...], s_new), axis=1, keepdims=True)
        p_new = jnp.exp2(s_new - m)
        l = jnp.sum(p_new, axis=1, keepdims=True)
        acc = _dot(p_new.astype(BF16), newp[:, 0:D_KV_LAT])
        for c in range(NJ):
            p = jnp.exp2(s_ref[c] - m)
            l = l + jnp.sum(p, axis=1, keepdims=True)
            acc = acc + _dot_nt(p.astype(BF16), lat_ref[c])
        o_lat = (acc / l).astype(BF16)
        o_all = _dot(o_lat, wuv_ref[...])
        lane_head = lax.broadcasted_iota(I32, (T, H_MLA * DV_MLA), 1) // DV_MLA
        out = jnp.zeros((T, H_MLA * DV_MLA), F32)
        for hh in range(H_MLA):
            out = out + jnp.where(lane_head == hh, o_all[T * hh:T * hh + T, :], 0.0)
        o_ref[...] = out


def _mla_sample(l, cache, pt_flat, n_pages, qc_hm, new_rows, wuv_cat, T):
    B = pt_flat.shape[0] // n_pages
    NP = min(MLA_PAGES_PER_STEP, n_pages)
    NJ = n_pages // NP
    R = H_MLA * T

    def page_spec(p):
        return pl.BlockSpec((None, None, MLA_ROW, PAGE),
                            lambda b, j, pt: (l, pt[b * n_pages + j * NP + p], 0, 0))
    in_specs = [page_spec(p) for p in range(NP)] + [
        pl.BlockSpec((H_MLA, T, MLA_ROW), lambda b, j, pt: (0, b, 0)),
        pl.BlockSpec((T, MLA_ROW), lambda b, j, pt: (b, 0)),
        pl.BlockSpec((D_KV_LAT, H_MLA * DV_MLA), lambda b, j, pt: (0, 0))]
    grid_spec = pltpu.PrefetchScalarGridSpec(
        num_scalar_prefetch=1, grid=(B, NJ), in_specs=in_specs,
        out_specs=pl.BlockSpec((T, H_MLA * DV_MLA), lambda b, j, pt: (b, 0)),
        scratch_shapes=[pltpu.VMEM((PAGE, MLA_ROW), BF16), pltpu.VMEM((NJ, D_KV_LAT, NP * PAGE), BF16),
                        pltpu.VMEM((NJ, R, NP * PAGE), F32), pltpu.VMEM((R, LANES), F32)])
    return pl.pallas_call(
        functools.partial(_mla_sample_kernel, NP, NJ, T),
        grid_spec=grid_spec,
        out_shape=jax.ShapeDtypeStruct((B * T, H_MLA * DV_MLA), F32),
        compiler_params=_cparams(("arbitrary", "arbitrary")),
    )(pt_flat, *([cache] * NP), qc_hm, new_rows, wuv_cat)


def _mix_kernel(oret_ref, odsa_ref, omla_ref, x_ref, g1_ref, sh2_ref, sc2_ref, n2_ref,
                wout_ref, wrh_ref, wrl_ref, br_ref, x1_ref, h2_ref, gate_ref):
    y = (_dot(oret_ref[...].astype(BF16), wout_ref[0:384, :])
         + _dot(odsa_ref[...].astype(BF16), wout_ref[384:640, :])
         + _dot(omla_ref[...].astype(BF16), wout_ref[640:1024, :]))
    x1 = x_ref[...] + g1_ref[...] * y
    x1_ref[...] = x1
    rstd = lax.rsqrt(jnp.mean(x1 * x1, axis=-1, keepdims=True) + EPS)
    h2 = (x1 * rstd) * n2_ref[...] * (1.0 + sc2_ref[...]) + sh2_ref[...]
    h2b = h2.astype(BF16)
    h2_ref[...] = h2b
    logits = _dot(h2b, wrh_ref[...]) + _dot(h2b, wrl_ref[...]) + br_ref[...]
    lane = lax.broadcasted_iota(I32, logits.shape, 1)
    is_g = (lane >= N_EXPERTS) & (lane < N_EXPERTS + N_GROUPS)
    gl = jnp.where(is_g, logits, -jnp.inf)
    gmax = jnp.max(gl, axis=-1, keepdims=True)
    grp = jnp.min(jnp.where(gl == gmax, lane - N_EXPERTS, N_GROUPS), axis=-1, keepdims=True)
    g_prob = 1.0 / jnp.sum(jnp.exp(gl - gmax), axis=-1, keepdims=True)
    in_grp = (lane < N_EXPERTS) & ((lane >> 3) == grp)
    el = jnp.where(in_grp, logits, -jnp.inf)
    m1 = jnp.max(el, axis=-1, keepdims=True)
    i1 = jnp.min(jnp.where(el == m1, lane, LANES), axis=-1, keepdims=True)
    el2 = jnp.where(lane == i1, -jnp.inf, el)
    m2 = jnp.max(el2, axis=-1, keepdims=True)
    i2 = jnp.min(jnp.where(el2 == m2, lane, LANES), axis=-1, keepdims=True)
    e2 = jnp.exp(m2 - m1)
    den = 1.0 + e2
    gate_ref[...] = jnp.where(lane == i1, 1.0 / den, jnp.where(lane == i2, e2 / den, 0.0)) * g_prob


def _mix(o_ret, o_dsa, o_mla, x, gate1, shift2, scale2, n2, lw, tm):
    T = x.shape[0]
    per_tok = gate1.shape[0] != 1

    def row(width):
        return pl.BlockSpec((tm, width), lambda i: (i, 0))

    def const(shape):
        return pl.BlockSpec(shape, lambda i: (0, 0))
    mod_spec = row(D_MODEL) if per_tok else const((1, D_MODEL))
    return pl.pallas_call(
        _mix_kernel, grid=(T // tm,),
        in_specs=[row(384), row(256), row(384), row(D_MODEL), mod_spec, mod_spec, mod_spec,
                  const((1, D_MODEL)), const((D_MODEL, D_MODEL)), const((D_MODEL, LANES)),
                  const((D_MODEL, LANES)), const((1, LANES))],
        out_specs=[row(D_MODEL), row(D_MODEL), row(LANES)],
        out_shape=[jax.ShapeDtypeStruct((T, D_MODEL), F32), jax.ShapeDtypeStruct((T, D_MODEL), BF16),
                   jax.ShapeDtypeStruct((T, LANES), F32)],
        compiler_params=_cparams(("arbitrary",)),
    )(o_ret, o_dsa, o_mla, x, gate1, shift2, scale2, n2, lw["w_out"], lw["w_rt_hi"], lw["w_rt_lo"], lw["b_rt"])


def _moe_kernel(h_ref, gate_ref, x_ref, g2_ref, w13_ref, w2_ref, o_ref, acc_ref):
    e = pl.program_id(1)

    @pl.when(e == 0)
    def _():
        acc_ref[...] = jnp.zeros_like(acc_ref)

    gate = gate_ref[...]
    lane = lax.broadcasted_iota(I32, gate.shape, 1)
    g_col = jnp.sum(jnp.where(lane == e, gate, 0.0), axis=-1, keepdims=True)
    au = _dot(h_ref[...], w13_ref[0])
    a, u = au[:, 0:D_EXPERT], au[:, D_EXPERT:]
    act = (a * _sigmoid(a)) * u * g_col
    acc_ref[...] += _dot(act.astype(BF16), w2_ref[0])

    @pl.when(e == pl.num_programs(1) - 1)
    def _():
        o_ref[...] = x_ref[...] + g2_ref[...] * acc_ref[...]


def _moe(h2, gate, x1, gate2, w13, w2, tm):
    T = h2.shape[0]
    per_tok = gate2.shape[0] != 1
    mod_spec = (pl.BlockSpec((tm, D_MODEL), lambda i, e: (i, 0)) if per_tok
                else pl.BlockSpec((1, D_MODEL), lambda i, e: (0, 0)))
    return pl.pallas_call(
        _moe_kernel, grid=(T // tm, N_EXPERTS),
        in_specs=[pl.BlockSpec((tm, D_MODEL), lambda i, e: (i, 0)),
                  pl.BlockSpec((tm, LANES), lambda i, e: (i, 0)),
                  pl.BlockSpec((tm, D_MODEL), lambda i, e: (i, 0)),
                  mod_spec,
                  pl.BlockSpec((1, D_MODEL, 2 * D_EXPERT), lambda i, e: (e, 0, 0)),
                  pl.BlockSpec((1, D_EXPERT, D_MODEL), lambda i, e: (e, 0, 0))],
        out_specs=pl.BlockSpec((tm, D_MODEL), lambda i, e: (i, 0)),
        out_shape=jax.ShapeDtypeStruct((T, D_MODEL), F32),
        scratch_shapes=[pltpu.VMEM((tm, D_MODEL), F32)],
        compiler_params=_cparams(("arbitrary", "arbitrary")),
    )(h2, gate, x1, gate2, w13, w2)


def _rope_tables(pos, d):
    half = d // 2
    inv = ROPE_THETA ** (-jnp.arange(half, dtype=F32) * 2.0 / d)
    ang = pos.astype(F32)[:, None] * inv[None, :]
    cos, sin = jnp.cos(ang), jnp.sin(ang)
    reps = LANES // d
    return (jnp.tile(jnp.concatenate([cos, cos], axis=1), (1, reps)),
            jnp.tile(jnp.concatenate([-sin, sin], axis=1), (1, reps)))


def _pad_cols(w, width):
    return jnp.pad(w, ((0, 0), (0, width - w.shape[1])))


def _layer_weights(l, w_in, ret_gn_g, dsa_qn_g, dsa_kn_g, idx_kn_g, mla_cqn_g, mla_ckvn_g, mla_krn_g,
                   mla_qn_g, w_uq, w_uk, w_uv, w_out, w_group, b_group, w_router, b_router, w1, w3, w2):
    offs = np.cumsum((0,) + IN_SPLITS)
    cols = [w_in[l][:, offs[k]:offs[k + 1]] for k in range(len(IN_SPLITS))]
    rq, rk, rv, rg, dq, dk, dv, iq, ik, iw, cq, ckv, kr = cols
    w_pack = jnp.concatenate([rq, rk, rv, rg, dq, dk, dv, iq, _pad_cols(ik, 128), cq, ckv,
                              _pad_cols(kr, 128), _pad_cols(iw, 128)], axis=1).astype(BF16)
    uq = w_uq[l].reshape(D_Q_LAT, H_MLA, D_QK_MLA)
    uq_pack = jnp.concatenate([uq[:, :, :D_NOPE].reshape(D_Q_LAT, 384), uq[:, :, D_NOPE:].reshape(D_Q_LAT, 192),
                               jnp.zeros((D_Q_LAT, 64), F32)], axis=1).astype(BF16)
    ukbd = jnp.zeros((384, 768), F32)
    for hh in range(H_MLA):
        ukbd = ukbd.at[64 * hh:64 * hh + 64, 128 * hh:128 * hh + 128].set(w_uk[l, hh])
    g_mq = jnp.concatenate([jnp.tile(mla_qn_g[l, :D_NOPE], H_MLA), jnp.tile(mla_qn_g[l, D_NOPE:], H_MLA),
                            jnp.zeros((64,), F32)])[None, :]
    head_of = np.concatenate([np.repeat(np.arange(H_MLA), D_NOPE), np.repeat(np.arange(H_MLA), D_ROPE),
                              np.full((64,), -1)])
    bq = jnp.asarray((head_of[:, None] == head_of[None, :]) & (head_of[:, None] >= 0), BF16)
    h64 = np.arange(256) // 64
    b64 = jnp.asarray(h64[:, None] == h64[None, :], BF16)
    w_rt = jnp.concatenate([w_router[l], w_group[l], jnp.zeros((D_MODEL, LANES - N_EXPERTS - N_GROUPS), F32)], axis=1)
    rt_hi = w_rt.astype(BF16)
    rt_lo = (w_rt - rt_hi.astype(F32)).astype(BF16)
    b_rt = jnp.concatenate([b_router[l], b_group[l], jnp.zeros((LANES - N_EXPERTS - N_GROUPS,), F32)])[None, :]
    return dict(
        w_pack=w_pack, w_uq=uq_pack, w_ukbd=ukbd.astype(BF16), b64=b64, bq=bq,
        g_dq=jnp.tile(dsa_qn_g[l], H_DSA)[None, :],
        g_kv=_pad_cols(dsa_kn_g[l][None, :], 128), g_ik=_pad_cols(idx_kn_g[l][None, :], 128),
        g_cq=mla_cqn_g[l][None, :], g_ckv=mla_ckvn_g[l][None, :], g_kr=_pad_cols(mla_krn_g[l][None, :], 128),
        g_mq=g_mq, gn=ret_gn_g[l].reshape(1, H_RET * DV_RET),
        wuvt=jnp.swapaxes(w_uv[l], 1, 2).astype(BF16),
        wuv_cat=jnp.transpose(w_uv[l], (1, 0, 2)).reshape(D_KV_LAT, H_MLA * DV_MLA).astype(BF16),
        w_out=w_out[l].astype(BF16), w_rt_hi=rt_hi, w_rt_lo=rt_lo, b_rt=b_rt,
        w13=jnp.concatenate([w1[l], w3[l]], axis=-1).astype(BF16), w2=w2[l].astype(BF16))


def _pick_tm(T, pref):
    tm = min(pref, T)
    while T % tm:
        tm //= 2
    return tm


def kernel(x_prompt, x_sample, cache_dsa, cache_mla, state_ret, page_table, c_prompt, c_sample, norm1_g, norm2_g, w_ada, b_ada, w_in, ret_gn_g, dsa_qn_g, dsa_kn_g, idx_kn_g, mla_cqn_g, mla_ckvn_g, mla_krn_g, mla_qn_g, w_uq, w_uk, w_uv, w_out, w_group, b_group, w_router, b_router, w1, w3, w2):
    depth = w_in.shape[0]
    BP, S, _ = x_prompt.shape
    B, T, _ = x_sample.shape
    assert BP == 1 and S % 256 == 0 and T == 8
    n_pages = page_table.shape[1]
    past = n_pages * PAGE
    topk_p = min(TOPK_MAX, S // 4)
    topk_s = min(TOPK_MAX, (past + T) // 4)
    log_gammas = tuple(float(np.log(np.float32(1.0) - np.float32(2.0) ** np.float32(-5.0 - h))) for h in range(H_RET))
    KC = _pick_tm(S, 512)
    assert S % (KC * UNROLL) == 0 and S < ZERO_ATOM and past + PAGE < ZERO_ATOM
    cache_dsa_t = jnp.swapaxes(cache_dsa, 2, 3)
    cache_mla_t = jnp.swapaxes(cache_mla, 2, 3)

    n_rows = 1 + B
    r_pad = (-n_rows) % 8
    c_all = jnp.concatenate([c_prompt, c_sample, jnp.zeros((r_pad, D_MODEL), F32)], axis=0)
    mods = _ada(c_all, w_ada, b_ada)

    tabs_p = _rope_tables(jnp.arange(S), 64) + _rope_tables(jnp.arange(S), 32)
    pos_s = jnp.tile(past + jnp.arange(T), B)
    tabs_s = _rope_tables(pos_s, 64) + _rope_tables(pos_s, 32)
    pt_flat = page_table.reshape(-1)

    xp = x_prompt.reshape(S, D_MODEL)
    xs = x_sample.reshape(B * T, D_MODEL)
    tm_p = KC
    tm_s = _pick_tm(B * T, 256)
    zero_state = jnp.zeros((1, H_RET, DK_RET, DV_RET), F32)
    outs = [[] for _ in range(6)]
    for l in range(depth):
        lw = _layer_weights(l, w_in, ret_gn_g, dsa_qn_g, dsa_kn_g, idx_kn_g, mla_cqn_g, mla_ckvn_g, mla_krn_g,
                            mla_qn_g, w_uq, w_uk, w_uv, w_out, w_group, b_group, w_router, b_router, w1, w3, w2)
        mp = [mods[l, 0:1, k * D_MODEL:(k + 1) * D_MODEL] for k in range(6)]
        ms = [jnp.repeat(mods[l, 1:1 + B, k * D_MODEL:(k + 1) * D_MODEL], T, axis=0) for k in range(6)]
        n1 = norm1_g[l][None, :]
        n2 = norm2_g[l][None, :]

        (rq, rk, rv, rg, dq_hm, iq_hm, _, iwt, drow, kd, ikd, kvt, qc_hm, mrow, kvm, ckvt) = _proj(
            xp, mp[0], mp[1], n1, tabs_p, lw, BF16, tm_p)
        o_ret, st_p = _retention(rq, rk, rv, rg, lw["gn"], zero_state, 128, log_gammas, BF16)
        o_dsa = _dsa_prompt(iq_hm, iwt, dq_hm, ikd, kd, kvt, topk_p, KC)
        o_mla = _mla_prompt(qc_hm, kvm, ckvt, lw["wuvt"], KC)
        x1, h2, gate = _mix(o_ret, o_dsa, o_mla, xp, mp[2], mp[3], mp[4], n2, lw, tm_p)
        xp = _moe(h2, gate, x1, mp[5], lw["w13"], lw["w2"], _pick_tm(S, 1024))
        outs[0].append(drow.reshape(1, S, DSA_ROW))
        outs[2].append(mrow.reshape(1, S, MLA_ROW))
        outs[4].append(st_p)

        (rq, rk, rv, rg, dq_hm, iq_hm, iw, _, drow, _, _, _, qc_hm, mrow, _, _) = _proj(
            xs, ms[0], ms[1], n1, tabs_s, lw, F32, tm_s)
        o_ret, st_s = _retention(rq, rk, rv, rg, lw["gn"], state_ret[l].astype(F32), T, log_gammas, F32)
        o_dsa = _dsa_sample(l, cache_dsa_t, pt_flat, n_pages, iq_hm, iw, dq_hm, drow, T, topk_s)
        o_mla = _mla_sample(l, cache_mla_t, pt_flat, n_pages, qc_hm, mrow, lw["wuv_cat"], T)
        x1, h2, gate = _mix(o_ret, o_dsa, o_mla, xs, ms[2], ms[3], ms[4], n2, lw, tm_s)
        xs = _moe(h2, gate, x1, ms[5], lw["w13"], lw["w2"], _pick_tm(B * T, 1024))
        outs[1].append(drow.reshape(B, T, DSA_ROW))
        outs[3].append(mrow.reshape(B, T, MLA_ROW))
        outs[5].append(st_s)

    return (xp.reshape(1, S, D_MODEL), xs.reshape(B, T, D_MODEL),
            jnp.stack(outs[0]), jnp.stack(outs[1]), jnp.stack(outs[2]), jnp.stack(outs[3]),
            jnp.stack(outs[4]), jnp.stack(outs[5]))
```

```python
import functools
import math

import numpy as np
import jax
import jax.numpy as jnp
from jax import lax
from jax.experimental import pallas as pl
from jax.experimental.pallas import tpu as pltpu

F32 = jnp.float32
BF16 = jnp.bfloat16
I32 = jnp.int32

D_MODEL = 1024
PAGE = 128
H_RET, DK_RET, DV_RET = 6, 64, 64
H_DSA, HD_DSA = 4, 64
H_IDX, D_IDX = 8, 64
TOPK_MAX = 256
H_MLA, D_Q_LAT, D_KV_LAT, D_NOPE, D_ROPE, DV_MLA = 6, 256, 128, 64, 32, 64
N_GROUPS, EPG, N_EXPERTS, D_EXPERT = 4, 8, 32, 256
Q_BLOCK = 128
ROPE_THETA = 10000.0
EPS = 1e-6
DSA_ROW = 2 * HD_DSA + D_IDX
MLA_ROW = D_KV_LAT + D_ROPE
D_QK_MLA = D_NOPE + D_ROPE
IN_SPLITS = (384, 384, 384, 384, 256, 64, 64, 512, 64, 8, 256, 128, 32)

C_RQ, C_RK, C_RV, C_RG = 0, 384, 768, 1152
C_DQ, C_KV, C_IQ, C_IK, C_CQ, C_CKV, C_KR, C_IW = 1536, 1792, 1920, 2432, 2560, 2816, 2944, 3072
D_PACK = 3200
MLA_QW = 640

LANES = 128
VMEM_LIMIT = 56 * 1024 * 1024

INT_MIN = -2 ** 31
NEG_BIG = -1e30
LOG2E = math.log2(math.e)
CKVT_ROWS = D_KV_LAT + 16
UNROLL = 2
DSA_PAGES_PER_STEP = 64
MLA_PAGES_PER_STEP = 64
ZERO_ATOM = 1 << 15
KEY_NEG_INF = (0xFF800000 ^ 0x7FFFFFFF) - 2 ** 32


def _cparams(sem):
    return pltpu.CompilerParams(dimension_semantics=sem, vmem_limit_bytes=VMEM_LIMIT)


def _dot(a, b):
    return jnp.dot(a, b, preferred_element_type=F32)


def _dot_nt(a, b):
    return lax.dot_general(a, b, (((1,), (1,)), ((), ())), preferred_element_type=F32)


def _split(x):
    hi = x.astype(BF16)
    lo = (x - hi.astype(F32)).astype(BF16)
    return hi, lo


def _dot_f32_lhs(x, b_bf16):
    hi, lo = _split(x)
    return _dot(hi, b_bf16) + _dot(lo, b_bf16)


def _sigmoid(x):
    return 1.0 / (1.0 + jnp.exp(-x))


def _sort_key(x):
    b = lax.bitcast_convert_type(x, I32)
    return b ^ ((b >> 31) & 0x7FFFFFFF)


def _rope(x, cos, sin, half):
    w = x.shape[-1]
    fwd = pltpu.roll(x, w - half, axis=1)
    bwd = pltpu.roll(x, half, axis=1)
    lane = lax.broadcasted_iota(I32, x.shape, 1)
    first = (lane & (2 * half - 1)) < half
    return x * cos + jnp.where(first, fwd, bwd) * sin


def _tile_lanes(t, n):
    return t if n == 1 else jnp.concatenate([t] * n, axis=1)


def _ada_kernel(c_ref, w_ref, b_ref, o_ref):
    c = c_ref[...]
    a = c * _sigmoid(c)
    ah, al = _split(a)
    w = w_ref[0]
    wh, wl = _split(w)
    o_ref[0] = _dot(ah, wh) + _dot(ah, wl) + _dot(al, wh) + b_ref[0]


def _ada(c_all, w_ada, b_ada):
    L = w_ada.shape[0]
    R = c_all.shape[0]
    tn = 1024
    return pl.pallas_call(
        _ada_kernel,
        grid=(L, 6 * D_MODEL // tn),
        in_specs=[pl.BlockSpec((R, D_MODEL), lambda l, j: (0, 0)),
                  pl.BlockSpec((1, D_MODEL, tn), lambda l, j: (l, 0, j)),
                  pl.BlockSpec((1, 1, tn), lambda l, j: (l, 0, j))],
        out_specs=pl.BlockSpec((1, R, tn), lambda l, j: (l, 0, j)),
        out_shape=jax.ShapeDtypeStruct((L, R, 6 * D_MODEL), F32),
        compiler_params=_cparams(("arbitrary", "arbitrary")),
    )(c_all, w_ada, b_ada.reshape(L, 1, 6 * D_MODEL))


def _proj_kernel(x_ref, sh_ref, sc_ref, g1_ref, c64_ref, s64_ref, c32_ref, s32_ref,
                 w_ref, gq_ref, gkv_ref, gik_ref, gcq_ref, gckv_ref, gkr_ref, gmq_ref,
                 wuq_ref, wuk_ref, b64_ref, bq_ref,
                 rq_ref, rk_ref, rv_ref, rg_ref, dq_ref, iq_ref, iw_ref, iwt_ref,
                 drow_ref, kd_ref, ikd_ref, kvt_ref, qc_ref, mrow_ref, kvm_ref, ckvt_ref):
    x = x_ref[...]
    rstd = lax.rsqrt(jnp.mean(x * x, axis=-1, keepdims=True) + EPS)
    h = (x * rstd) * g1_ref[...] * (1.0 + sc_ref[...]) + sh_ref[...]
    hb = h.astype(BF16)
    c64, s64, c32, s32 = c64_ref[...], s64_ref[...], c32_ref[...], s32_ref[...]

    def zcols(c0, width):
        return _dot(hb, w_ref[:, c0:c0 + width])

    rq_ref[...] = _rope(zcols(C_RQ, 384), _tile_lanes(c64, 3), _tile_lanes(s64, 3), 32)
    rk_ref[...] = _rope(zcols(C_RK, 384), _tile_lanes(c64, 3), _tile_lanes(s64, 3), 32) * (DK_RET ** -0.5)
    rv_ref[...] = zcols(C_RV, 384)
    rg_ref[...] = zcols(C_RG, 384)

    dq = zcols(C_DQ, 256)
    ss = _dot_f32_lhs(dq * dq, b64_ref[...])
    dq = dq * lax.rsqrt(ss * (1.0 / HD_DSA) + EPS) * gq_ref[...]
    dq = _rope(dq, _tile_lanes(c64, 2), _tile_lanes(s64, 2), 32) * (HD_DSA ** -0.5 * LOG2E)
    for hh in range(H_DSA):
        dq_ref[hh] = dq[:, 64 * hh:64 * hh + 64].astype(dq_ref.dtype)

    kv = zcols(C_KV, 128)
    lane = lax.broadcasted_iota(I32, kv.shape, 1)
    is_k = lane < HD_DSA
    ssk = jnp.sum(jnp.where(is_k, kv * kv, 0.0), axis=-1, keepdims=True)
    kn = kv * lax.rsqrt(ssk * (1.0 / HD_DSA) + EPS) * gkv_ref[...]
    kv = jnp.where(is_k, _rope(kn, c64, s64, 32), kv)
    drow_ref[:, 0:128] = kv
    kvb = kv.astype(BF16)
    kd_ref[...] = kvb[:, 0:64]
    kvt_ref[0] = jnp.where(is_k, 1.0, kv).T.astype(BF16)

    iq = _rope(zcols(C_IQ, 512), _tile_lanes(c64, 4), _tile_lanes(s64, 4), 32) * (D_IDX ** -0.5)
    for hh in range(H_IDX):
        iq_ref[hh] = iq[:, 64 * hh:64 * hh + 64].astype(iq_ref.dtype)
    ik = zcols(C_IK, 128)
    ssi = jnp.sum(ik * ik, axis=-1, keepdims=True)
    ik = _rope(ik * lax.rsqrt(ssi * (1.0 / D_IDX) + EPS) * gik_ref[...], c64, s64, 32)
    drow_ref[:, 128:192] = ik[:, 0:64]
    ikd_ref[...] = ik[:, 0:64].astype(BF16)
    iw = zcols(C_IW, 128) * (H_IDX ** -0.5)
    iw_ref[...] = iw
    iwt_ref[...] = iw.T[0:H_IDX, :]

    cq = zcols(C_CQ, 256)
    cq = cq * lax.rsqrt(jnp.mean(cq * cq, axis=-1, keepdims=True) + EPS) * gcq_ref[...]
    q = _dot(cq.astype(BF16), wuq_ref[...])
    ssq = _dot_f32_lhs(q * q, bq_ref[...])
    q = q * lax.rsqrt(ssq * (1.0 / D_QK_MLA) + EPS) * gmq_ref[...]
    scale = D_QK_MLA ** -0.5 * LOG2E
    q_lat = _dot(q[:, 0:384].astype(BF16), wuk_ref[...]) * scale
    q_rope = _rope(q[:, 384:640], _tile_lanes(c32, 2), _tile_lanes(s32, 2), 16) * scale
    for hh in range(H_MLA):
        qc_ref[hh, :, 0:128] = q_lat[:, 128 * hh:128 * hh + 128].astype(qc_ref.dtype)
        qc_ref[hh, :, 128:160] = q_rope[:, 32 * hh:32 * hh + 32].astype(qc_ref.dtype)
    ckv = zcols(C_CKV, 128)
    ckv = ckv * lax.rsqrt(jnp.mean(ckv * ckv, axis=-1, keepdims=True) + EPS) * gckv_ref[...]
    kr = zcols(C_KR, 128)
    ssr = jnp.sum(kr * kr, axis=-1, keepdims=True)
    kr = _rope(kr * lax.rsqrt(ssr * (1.0 / D_ROPE) + EPS) * gkr_ref[...], c32, s32, 16)
    mrow_ref[:, 0:128] = ckv
    mrow_ref[:, 128:160] = kr[:, 0:32]
    kvm_ref[:, 0:128] = ckv.astype(BF16)
    kvm_ref[:, 128:160] = kr[:, 0:32].astype(BF16)
    ckvt_ref[0, 0:D_KV_LAT, :] = ckv.T.astype(BF16)
    ckvt_ref[0, D_KV_LAT:CKVT_ROWS, :] = jnp.ones((CKVT_ROWS - D_KV_LAT, ckv.shape[0]), BF16)


def _proj(x, shift, scale, g1, tabs, lw, qdtype, tm):
    T = x.shape[0]
    nb = T // tm
    per_tok = shift.shape[0] != 1

    def row(width):
        return pl.BlockSpec((tm, width), lambda i: (i, 0))

    def const(shape):
        nd = len(shape)
        return pl.BlockSpec(shape, lambda i: (0,) * nd)

    mod_spec = row(D_MODEL) if per_tok else const((1, D_MODEL))
    in_specs = [row(D_MODEL), mod_spec, mod_spec, const((1, D_MODEL)),
                row(LANES), row(LANES), row(LANES), row(LANES),
                const((D_MODEL, D_PACK)),
                const((1, 256)), const((1, 128)), const((1, 128)), const((1, 256)),
                const((1, 128)), const((1, 128)), const((1, MLA_QW)),
                const((D_Q_LAT, MLA_QW)), const((384, 768)), const((256, 256)), const((MLA_QW, MLA_QW))]
    out_shape = [
        jax.ShapeDtypeStruct((T, 384), F32), jax.ShapeDtypeStruct((T, 384), F32),
        jax.ShapeDtypeStruct((T, 384), F32), jax.ShapeDtypeStruct((T, 384), F32),
        jax.ShapeDtypeStruct((H_DSA, T, 64), qdtype), jax.ShapeDtypeStruct((H_IDX, T, 64), qdtype),
        jax.ShapeDtypeStruct((T, LANES), F32), jax.ShapeDtypeStruct((H_IDX, T), F32),
        jax.ShapeDtypeStruct((T, DSA_ROW), F32), jax.ShapeDtypeStruct((T, 64), BF16),
        jax.ShapeDtypeStruct((T, 64), BF16), jax.ShapeDtypeStruct((nb, 128, tm), BF16),
        jax.ShapeDtypeStruct((H_MLA, T, 160), qdtype), jax.ShapeDtypeStruct((T, MLA_ROW), F32),
        jax.ShapeDtypeStruct((T, MLA_ROW), BF16), jax.ShapeDtypeStruct((nb, CKVT_ROWS, tm), BF16)]
    out_specs = [
        row(384), row(384), row(384), row(384),
        pl.BlockSpec((H_DSA, tm, 64), lambda i: (0, i, 0)), pl.BlockSpec((H_IDX, tm, 64), lambda i: (0, i, 0)),
        row(LANES), pl.BlockSpec((H_IDX, tm), lambda i: (0, i)),
        row(DSA_ROW), row(64), row(64), pl.BlockSpec((1, 128, tm), lambda i: (i, 0, 0)),
        pl.BlockSpec((H_MLA, tm, 160), lambda i: (0, i, 0)), row(MLA_ROW), row(MLA_ROW),
        pl.BlockSpec((1, CKVT_ROWS, tm), lambda i: (i, 0, 0))]
    return pl.pallas_call(
        _proj_kernel, grid=(nb,), in_specs=in_specs, out_specs=out_specs, out_shape=out_shape,
        compiler_params=_cparams(("arbitrary",)),
    )(x, shift, scale, g1, *tabs, lw["w_pack"], lw["g_dq"], lw["g_kv"], lw["g_ik"], lw["g_cq"],
      lw["g_ckv"], lw["g_kr"], lw["g_mq"], lw["w_uq"], lw["w_ukbd"], lw["b64"], lw["bq"])


def _ret_kernel(log_gammas, C, q_ref, k_ref, v_ref, g_ref, gn_ref, s0_ref, o_ref, st_ref):
    first = pl.program_id(1) == 0

    @pl.when(first)
    def _():
        st_ref[...] = s0_ref[...]

    ri = lax.broadcasted_iota(I32, (C, C), 0)
    ci = lax.broadcasted_iota(I32, (C, C), 1)
    rel = (ri - ci).astype(F32)
    pos = lax.broadcasted_iota(I32, (C, 1), 0).astype(F32)
    q_all, k_all, v_all, g_all, gn = q_ref[...], k_ref[...], v_ref[...], g_ref[...], gn_ref[...]
    for hh in range(H_RET):
        lg = log_gammas[hh]
        sl = slice(64 * hh, 64 * hh + 64)
        q, k, v = q_all[:, sl], k_all[:, sl], v_all[:, sl]
        qb, vb = q.astype(BF16), v.astype(BF16)
        decay = jnp.where(rel >= 0, jnp.exp(lg * jnp.maximum(rel, 0.0)), 0.0)
        attn = _dot_nt(qb, k.astype(BF16)) * decay
        state = st_ref[0, hh]
        o = _dot(attn.astype(BF16), vb) + _dot(qb, state.astype(BF16)) * jnp.exp(lg * (pos + 1.0))
        k_dec = k * jnp.exp(lg * (C - 1.0 - pos))
        st_ref[0, hh] = state * math.exp(lg * C) + _dot(k_dec.T.astype(BF16), vb)
        mu = jnp.mean(o, axis=-1, keepdims=True)
        d = o - mu
        var = jnp.mean(d * d, axis=-1, keepdims=True)
        gate = g_all[:, sl]
        o_ref[:, sl] = (d * lax.rsqrt(var + EPS) * gn[:, sl] * (gate * _sigmoid(gate))).astype(o_ref.dtype)


def _retention(rq, rk, rv, rg, gn, state0, C, log_gammas, out_dtype):
    NB = state0.shape[0]
    T = rq.shape[0]
    nc = T // (NB * C)
    row = pl.BlockSpec((C, 384), lambda b, c: (b * nc + c, 0))
    st = pl.BlockSpec((1, H_RET, DK_RET, DV_RET), lambda b, c: (b, 0, 0, 0))
    return pl.pallas_call(
        functools.partial(_ret_kernel, log_gammas, C),
        grid=(NB, nc),
        in_specs=[row, row, row, row, pl.BlockSpec((1, 384), lambda b, c: (0, 0)), st],
        out_specs=[row, st],
        out_shape=[jax.ShapeDtypeStruct((T, 384), out_dtype),
                   jax.ShapeDtypeStruct((NB, H_RET, DK_RET, DV_RET), F32)],
        compiler_params=_cparams(("arbitrary", "arbitrary")),
    )(rq, rk, rv, rg, gn, state0)


def _count_ge(keys_ref, nch, kc, cand):
    def body(c, acc):
        blk = keys_ref[pl.ds(pl.multiple_of(c * kc, kc), kc), :]
        ind = jnp.where(blk >= cand, 1, 0).astype(I32)
        return acc + jnp.sum(ind.reshape(kc // 8, 8, LANES), axis=0)
    acc = lax.fori_loop(0, nch, body, jnp.zeros((8, LANES), I32))
    return jnp.sum(acc, axis=0, keepdims=True)


def _index_key(score, pos):
    k = _sort_key(score)
    return jnp.where(k > 0, k + ZERO_ATOM, jnp.where(k >= -1, ZERO_ATOM - pos, k))


def _index_key_of_float(x):
    k = _sort_key(x)
    return jnp.where(k > 0, k + ZERO_ATOM, k)


def _index_key_to_float(k):
    k = jnp.where(k > ZERO_ATOM, k - ZERO_ATOM, jnp.where(k > 0, 0, k))
    return lax.bitcast_convert_type(k ^ ((k >> 31) & 0x7FFFFFFF), F32)


def _index_key_range(fmin, fmax):
    kmin, kmax = _sort_key(fmin), _sort_key(fmax)
    return (jnp.where(kmin > 0, kmin + ZERO_ATOM, jnp.where(kmin >= -1, 1, kmin)),
            jnp.where(kmax > 0, kmax + ZERO_ATOM, jnp.where(kmax >= -1, ZERO_ATOM, kmax)))


def _kth_largest_key(count_fn, topk, n_valid, kmin, kmax):
    tgt = math.log2(topk)

    def lg(c):
        return jnp.log2(jnp.maximum(c.astype(F32), 0.5))

    def active(lo, hi, c_lo):
        return (c_lo > topk) & (hi > lo + 1)

    def cond(st):
        it, lo, hi, c_lo = st[:4]
        return (it < 128) & (jnp.max(jnp.where(active(lo, hi, c_lo), 1, 0)) > 0)

    def body(st):
        it, lo, hi, c_lo, c_hi, l_lo, l_hi, side = st
        f_lo, f_hi = _index_key_to_float(lo), _index_key_to_float(hi)
        by_score = _index_key_of_float(f_lo + (f_hi - f_lo) * ((l_lo - tgt) / (l_lo - l_hi)))
        frac = (c_lo - topk).astype(F32) / jnp.maximum(c_lo - c_hi, 1).astype(F32)
        by_pos = lo + 1 + ((hi - lo).astype(F32) * frac).astype(I32)
        cand = jnp.where((lo >= 1) & (hi <= ZERO_ATOM + 1), by_pos, by_score)
        cand = jnp.where(it == 0, ZERO_ATOM + 1, jnp.where(it == 1, 1, cand))
        mid = (lo >> 1) + (hi >> 1) + (lo & hi & 1)
        cand = jnp.where((cand > lo) & (cand < hi), cand, mid)
        cand = jnp.where((it >= 14) & ((it & 1) == 0), mid, cand)
        c = count_fn(cand)
        act = active(lo, hi, c_lo)
        up = act & (c >= topk)
        dn = act & (c < topk)
        lc = lg(c)
        l_lo_n = jnp.where(up, lc, jnp.where(dn & (side < 0), tgt + 0.5 * (l_lo - tgt), l_lo))
        l_hi_n = jnp.where(dn, lc, jnp.where(up & (side > 0), tgt + 0.5 * (l_hi - tgt), l_hi))
        return (it + 1, jnp.where(up, cand, lo), jnp.where(dn, cand, hi), jnp.where(up, c, c_lo),
                jnp.where(dn, c, c_hi), l_lo_n, l_hi_n, jnp.where(up, 1, jnp.where(dn, -1, side)))
    zero = jnp.zeros(kmin.shape, I32)
    init = (jnp.int32(0), kmin, kmax + 1, n_valid, zero, lg(n_valid), jnp.full(kmin.shape, -1.0, F32), zero)
    return lax.while_loop(cond, body, lax.fori_loop(0, 12, lambda _, st: body(st), init))[1]


def _dsa_prompt_kernel(topk, kc, iq_ref, iwt_ref, dq_ref, ikd_ref, kd_ref, kvt_ref, o_ref, keys_ref):
    i = pl.program_id(0)
    sup = kc * UNROLL
    last = (i * Q_BLOCK) // sup
    iqs = iq_ref[...].reshape(H_IDX * Q_BLOCK, D_IDX)
    iwt = iwt_ref[...]

    def idx_scores(start):
        s = _dot_nt(ikd_ref[pl.ds(start, kc), :], iqs)
        score = jnp.maximum(s[:, 0:LANES], 0.0) * iwt[0:1, :]
        for hh in range(1, H_IDX):
            score = score + jnp.maximum(s[:, LANES * hh:LANES * hh + LANES], 0.0) * iwt[hh:hh + 1, :]
        return score

    def fold(x, op):
        return op(x.reshape(kc // 8, 8, LANES), axis=0)

    def idx_body(c, carry):
        mx, mn = carry
        for u in range(UNROLL):
            start = pl.multiple_of((c * UNROLL + u) * kc, kc)
            sc = idx_scores(start)
            keys_ref[pl.ds(start, kc), :] = _index_key(sc, start + row_iota)
            mx, mn = jnp.maximum(mx, fold(sc, jnp.max)), jnp.minimum(mn, fold(sc, jnp.min))
        return mx, mn
    row_iota = lax.broadcasted_iota(I32, (kc, LANES), 0)
    mx, mn = lax.fori_loop(0, last, idx_body, (jnp.full((8, LANES), -jnp.inf, F32), jnp.full((8, LANES), jnp.inf, F32)))
    q_pos = i * Q_BLOCK + lax.broadcasted_iota(I32, (kc, LANES), 1)
    for u in range(UNROLL):
        start = pl.multiple_of((last * UNROLL + u) * kc, kc)
        valid = start + row_iota <= q_pos
        sc = idx_scores(start)
        sc_lo = jnp.where(valid, sc, -jnp.inf)
        keys_ref[pl.ds(start, kc), :] = _index_key(sc_lo, start + row_iota)
        mx = jnp.maximum(mx, fold(sc_lo, jnp.max))
        mn = jnp.minimum(mn, fold(jnp.where(valid, sc, jnp.inf), jnp.min))

    kmin, kmax = _index_key_range(jnp.min(mn, axis=0, keepdims=True), jnp.max(mx, axis=0, keepdims=True))
    n_valid = i * Q_BLOCK + lax.broadcasted_iota(I32, (1, LANES), 1) + 1
    thr = _kth_largest_key(lambda cand: _count_ge(keys_ref, (last + 1) * UNROLL, kc, cand), topk, n_valid, kmin, kmax)

    dqs = dq_ref[...].reshape(H_DSA * Q_BLOCK, HD_DSA)
    W = H_DSA * LANES

    def att_step(cs, carry):
        m, acc = carry
        start = pl.multiple_of(cs * kc, kc)
        s = _dot_nt(kd_ref[pl.ds(start, kc), :], dqs)
        sel = keys_ref[pl.ds(start, kc), :] >= thr
        s = jnp.concatenate([jnp.where(sel, s[:, LANES * hh:LANES * hh + LANES], NEG_BIG)
                             for hh in range(H_DSA)], axis=1)
        m_new = jnp.maximum(m, jnp.max(s, axis=0, keepdims=True))
        p = jnp.exp2(s - m_new)
        acc = jnp.exp2(m - m_new) * acc + _dot(kvt_ref[cs], p.astype(BF16))
        return m_new, acc

    def att_body(c, carry):
        for u in range(UNROLL):
            carry = att_step(c * UNROLL + u, carry)
        return carry
    m0 = jnp.full((1, W), NEG_BIG, F32)
    m, acc = lax.fori_loop(0, last + 1, att_body, (m0, jnp.zeros((LANES, W), F32)))
    o_t = acc[HD_DSA:, :] / acc[0:1, :]
    o_hq = jnp.concatenate([o_t[:, LANES * hh:LANES * hh + LANES] for hh in range(H_DSA)], axis=0)
    o_ref[...] = o_hq.T.astype(o_ref.dtype)


def _dsa_prompt(iq_hm, iwt, dq_hm, ikd, kd, kvt, topk, kc):
    T = ikd.shape[0]
    nq = T // Q_BLOCK
    nkb = kvt.shape[0]
    return pl.pallas_call(
        functools.partial(_dsa_prompt_kernel, topk, kc),
        grid=(nq,),
        in_specs=[pl.BlockSpec((H_IDX, Q_BLOCK, D_IDX), lambda i: (0, i, 0)),
                  pl.BlockSpec((H_IDX, Q_BLOCK), lambda i: (0, i)),
                  pl.BlockSpec((H_DSA, Q_BLOCK, HD_DSA), lambda i: (0, i, 0)),
                  pl.BlockSpec((T, D_IDX), lambda i: (0, 0)),
                  pl.BlockSpec((T, HD_DSA), lambda i: (0, 0)),
                  pl.BlockSpec((nkb, 128, kc), lambda i: (0, 0, 0))],
        out_specs=pl.BlockSpec((Q_BLOCK, H_DSA * HD_DSA), lambda i: (i, 0)),
        out_shape=jax.ShapeDtypeStruct((T, H_DSA * HD_DSA), BF16),
        scratch_shapes=[pltpu.VMEM((nkb * kc, LANES), I32)],
        compiler_params=_cparams(("arbitrary",)),
    )(iq_hm, iwt, dq_hm, ikd, kd, kvt)


def _mla_prompt_kernel(kc, qc_ref, kvm_ref, ckvt_ref, wuvt_ref, o_ref):
    i = pl.program_id(0)
    last = (i * Q_BLOCK) // (kc * UNROLL)
    W = H_MLA * LANES
    qs = qc_ref[...].reshape(H_MLA * Q_BLOCK, MLA_ROW)

    def step(c, carry, causal):
        m, acc = carry
        start = pl.multiple_of(c * kc, kc)
        s = _dot_nt(kvm_ref[pl.ds(start, kc), :], qs)
        if causal:
            q_pos = i * Q_BLOCK + (lax.broadcasted_iota(I32, (kc, W), 1) & (LANES - 1))
            key_pos = start + lax.broadcasted_iota(I32, (kc, W), 0)
            s = jnp.where(key_pos <= q_pos, s, NEG_BIG)
        m_new = jnp.maximum(m, jnp.max(s, axis=0, keepdims=True))
        p = jnp.exp2(s - m_new)
        acc = jnp.exp2(m - m_new) * acc + _dot(ckvt_ref[c], p.astype(BF16))
        return m_new, acc
    carry = (jnp.full((1, W), NEG_BIG, F32), jnp.zeros((CKVT_ROWS, W), F32))

    def body(c, cr):
        for u in range(UNROLL):
            cr = step(c * UNROLL + u, cr, False)
        return cr
    carry = lax.fori_loop(0, last, body, carry)
    for u in range(UNROLL):
        carry = step(last * UNROLL + u, carry, True)
    m, acc = carry
    o_lat = (acc[0:D_KV_LAT, :] / acc[D_KV_LAT:D_KV_LAT + 1, :]).astype(BF16)
    outs = [_dot(wuvt_ref[hh], o_lat[:, LANES * hh:LANES * hh + LANES]) for hh in range(H_MLA)]
    o_ref[...] = jnp.concatenate(outs, axis=0).T.astype(o_ref.dtype)


def _mla_prompt(qc_hm, kvm, ckvt, wuvt, kc):
    T = kvm.shape[0]
    nq = T // Q_BLOCK
    nkb = ckvt.shape[0]
    return pl.pallas_call(
        functools.partial(_mla_prompt_kernel, kc),
        grid=(nq,),
        in_specs=[pl.BlockSpec((H_MLA, Q_BLOCK, MLA_ROW), lambda i: (0, i, 0)),
                  pl.BlockSpec((T, MLA_ROW), lambda i: (0, 0)),
                  pl.BlockSpec((nkb, CKVT_ROWS, kc), lambda i: (0, 0, 0)),
                  pl.BlockSpec((H_MLA, DV_MLA, D_KV_LAT), lambda i: (0, 0, 0))],
        out_specs=pl.BlockSpec((Q_BLOCK, H_MLA * DV_MLA), lambda i: (i, 0)),
        out_shape=jax.ShapeDtypeStruct((T, H_MLA * DV_MLA), BF16),
        compiler_params=_cparams(("arbitrary",)),
    )(qc_hm, kvm, ckvt, wuvt)


def _dsa_sample_kernel(topk, NP, NJ, T, *refs):
    pages = refs[1:1 + NP]
    iq_ref, iw_ref, dq_ref, new_ref, o_ref, rowst_ref, newp_ref, keys_ref, satt_ref, mm_ref = refs[1 + NP:]
    j = pl.program_id(1)
    CH = NP * PAGE
    iqs = iq_ref[...].reshape(H_IDX * T, D_IDX).astype(BF16)
    dqs = dq_ref[...].reshape(H_DSA * T, HD_DSA).astype(BF16)
    iw = iw_ref[...]

    def head_sum(s):
        score = jnp.maximum(s[0:T, :], 0.0) * iw[:, 0:1]
        for hh in range(1, H_IDX):
            score = score + jnp.maximum(s[T * hh:T * hh + T, :], 0.0) * iw[:, hh:hh + 1]
        return score

    def per_head(sel, s):
        return jnp.concatenate([jnp.where(sel, s[T * hh:T * hh + T, :], NEG_BIG) for hh in range(H_DSA)], axis=0)

    chunk = jnp.concatenate([pages[p][...] for p in range(NP)], axis=1).astype(BF16)
    rowst_ref[j] = chunk
    sc = head_sum(_dot(iqs, chunk[2 * HD_DSA:DSA_ROW, :]))
    keys_ref[j] = _index_key(sc, j * CH + lax.broadcasted_iota(I32, (T, CH), 1))
    satt_ref[j] = _dot(dqs, chunk[0:HD_DSA, :])

    def fold(x, op):
        return functools.reduce(op, [x[:, LANES * g:LANES * g + LANES] for g in range(x.shape[1] // LANES)])

    @pl.when(j == 0)
    def _():
        mm_ref[0] = jnp.full((T, LANES), -jnp.inf, F32)
        mm_ref[1] = jnp.full((T, LANES), jnp.inf, F32)
    mm_ref[0] = jnp.maximum(mm_ref[0], fold(sc, jnp.maximum))
    mm_ref[1] = jnp.minimum(mm_ref[1], fold(sc, jnp.minimum))

    @pl.when(j == NJ - 1)
    def _():
        newp_ref[...] = jnp.zeros((PAGE, DSA_ROW), BF16)
        newp_ref[0:T, :] = new_ref[...].astype(BF16)
        newp = newp_ref[...]
        qi = lax.broadcasted_iota(I32, (T, PAGE), 0)
        kj = lax.broadcasted_iota(I32, (T, PAGE), 1)
        idx_new = head_sum(_dot_nt(iqs, newp[:, 2 * HD_DSA:DSA_ROW]))
        sc_new = jnp.where(kj <= qi, idx_new, -jnp.inf)
        keys_new = _index_key(sc_new, NJ * CH + kj)
        fmax = jnp.max(jnp.maximum(mm_ref[0], sc_new), axis=1, keepdims=True)
        fmin = jnp.min(jnp.minimum(mm_ref[1], jnp.where(kj <= qi, idx_new, jnp.inf)), axis=1, keepdims=True)
        kmin, kmax = _index_key_range(fmin, fmax)
        n_valid = NJ * CH + 1 + lax.broadcasted_iota(I32, (T, 1), 0)
        s_new = _dot_nt(dqs, newp[:, 0:HD_DSA])

        def count_fn(cand):
            accs = [jnp.where(keys_new >= cand, 1, 0).astype(I32)] + [jnp.zeros((T, LANES), I32)] * 3
            for c in range(NJ):
                for g in range(CH // LANES):
                    blk = keys_ref[c, :, LANES * g:LANES * g + LANES]
                    accs[g % 4] = accs[g % 4] + jnp.where(blk >= cand, 1, 0).astype(I32)
            return jnp.sum((accs[0] + accs[1]) + (accs[2] + accs[3]), axis=1, keepdims=True)
        thr = _kth_largest_key(count_fn, topk, n_valid, kmin, kmax)

        sn = per_head(keys_new >= thr, s_new)
        m = jnp.max(sn, axis=1, keepdims=True)
        for c in range(NJ):
            m = jnp.maximum(m, jnp.max(per_head(keys_ref[c] >= thr, satt_ref[c]), axis=1, keepdims=True))
        pn = jnp.exp2(sn - m)
        l = jnp.sum(pn, axis=1, keepdims=True)
        acc = _dot(pn.astype(BF16), newp[:, HD_DSA:2 * HD_DSA])
        for c in range(NJ):
            p = jnp.exp2(per_head(keys_ref[c] >= thr, satt_ref[c]) - m)
            l = l + jnp.sum(p, axis=1, keepdims=True)
            acc = acc + _dot_nt(p.astype(BF16), rowst_ref[c, HD_DSA:2 * HD_DSA, :])
        o = acc / l
        for hh in range(H_DSA):
            o_ref[:, 64 * hh:64 * hh + 64] = o[T * hh:T * hh + T, :]


def _dsa_sample(l, cache, pt_flat, n_pages, iq_hm, iw, dq_hm, new_rows, T, topk):
    B = pt_flat.shape[0] // n_pages
    NP = min(DSA_PAGES_PER_STEP, n_pages)
    NJ = n_pages // NP
    CH = NP * PAGE

    def page_spec(p):
        return pl.BlockSpec((None, None, DSA_ROW, PAGE),
                            lambda b, j, pt: (l, pt[b * n_pages + j * NP + p], 0, 0))
    in_specs = [page_spec(p) for p in range(NP)] + [
        pl.BlockSpec((H_IDX, T, D_IDX), lambda b, j, pt: (0, b, 0)),
        pl.BlockSpec((T, LANES), lambda b, j, pt: (b, 0)),
        pl.BlockSpec((H_DSA, T, HD_DSA), lambda b, j, pt: (0, b, 0)),
        pl.BlockSpec((T, DSA_ROW), lambda b, j, pt: (b, 0))]
    grid_spec = pltpu.PrefetchScalarGridSpec(
        num_scalar_prefetch=1, grid=(B, NJ), in_specs=in_specs,
        out_specs=pl.BlockSpec((T, H_DSA * HD_DSA), lambda b, j, pt: (b, 0)),
        scratch_shapes=[pltpu.VMEM((NJ, DSA_ROW, CH), BF16), pltpu.VMEM((PAGE, DSA_ROW), BF16),
                        pltpu.VMEM((NJ, T, CH), I32), pltpu.VMEM((NJ, H_DSA * T, CH), F32),
                        pltpu.VMEM((2, T, LANES), F32)])
    return pl.pallas_call(
        functools.partial(_dsa_sample_kernel, topk, NP, NJ, T),
        grid_spec=grid_spec,
        out_shape=jax.ShapeDtypeStruct((B * T, H_DSA * HD_DSA), F32),
        compiler_params=_cparams(("arbitrary", "arbitrary")),
    )(pt_flat, *([cache] * NP), iq_hm, iw, dq_hm, new_rows)


def _mla_sample_kernel(NP, NJ, T, *refs):
    pages = refs[1:1 + NP]
    qc_ref, new_ref, wuv_ref, o_ref, newp_ref, lat_ref, s_ref, mx_ref = refs[1 + NP:]
    j = pl.program_id(1)
    R = H_MLA * T
    CH = NP * PAGE
    qs = qc_ref[...].reshape(R, MLA_ROW).astype(BF16)

    chunk = jnp.concatenate([pages[p][...] for p in range(NP)], axis=1).astype(BF16)
    lat_ref[j] = chunk[0:D_KV_LAT, :]
    s = _dot(qs, chunk)
    s_ref[j] = s
    smax = functools.reduce(jnp.maximum, [s[:, LANES * g:LANES * g + LANES] for g in range(CH // LANES)])

    @pl.when(j == 0)
    def _():
        mx_ref[...] = smax

    @pl.when(j > 0)
    def _():
        mx_ref[...] = jnp.maximum(mx_ref[...], smax)

    @pl.when(j == NJ - 1)
    def _():
        newp_ref[...] = jnp.zeros((PAGE, MLA_ROW), BF16)
        newp_ref[0:T, :] = new_ref[...].astype(BF16)
        newp = newp_ref[...]
        qi = lax.broadcasted_iota(I32, (R, PAGE), 0) & (T - 1)
        kj = lax.broadcasted_iota(I32, (R, PAGE), 1)
        s_new = jnp.where(kj <= qi, _dot_nt(qs, newp), NEG_BIG)
        m = jnp.max(jnp.maximum(mx_ref[...], s_new), axis=1, keepdims=True)
        p_new = jnp.exp2(s_new - m)
        l = jnp.sum(p_new, axis=1, keepdims=True)
        acc = _dot(p_new.astype(BF16), newp[:, 0:D_KV_LAT])
        for c in range(NJ):
            p = jnp.exp2(s_ref[c] - m)
            l = l + jnp.sum(p, axis=1, keepdims=True)
            acc = acc + _dot_nt(p.astype(BF16), lat_ref[c])
        o_lat = (acc / l).astype(BF16)
        o_all = _dot(o_lat, wuv_ref[...])
        lane_head = lax.broadcasted_iota(I32, (T, H_MLA * DV_MLA), 1) // DV_MLA
        out = jnp.zeros((T, H_MLA * DV_MLA), F32)
        for hh in range(H_MLA):
            out = out + jnp.where(lane_head == hh, o_all[T * hh:T * hh + T, :], 0.0)
        o_ref[...] = out


def _mla_sample(l, cache, pt_flat, n_pages, qc_hm, new_rows, wuv_cat, T):
    B = pt_flat.shape[0] // n_pages
    NP = min(MLA_PAGES_PER_STEP, n_pages)
    NJ = n_pages // NP
    R = H_MLA * T

    def page_spec(p):
        return pl.BlockSpec((None, None, MLA_ROW, PAGE),
                            lambda b, j, pt: (l, pt[b * n_pages + j * NP + p], 0, 0))
    in_specs = [page_spec(p) for p in range(NP)] + [
        pl.BlockSpec((H_MLA, T, MLA_ROW), lambda b, j, pt: (0, b, 0)),
        pl.BlockSpec((T, MLA_ROW), lambda b, j, pt: (b, 0)),
        pl.BlockSpec((D_KV_LAT, H_MLA * DV_MLA), lambda b, j, pt: (0, 0))]
    grid_spec = pltpu.PrefetchScalarGridSpec(
        num_scalar_prefetch=1, grid=(B, NJ), in_specs=in_specs,
        out_specs=pl.BlockSpec((T, H_MLA * DV_MLA), lambda b, j, pt: (b, 0)),
        scratch_shapes=[pltpu.VMEM((PAGE, MLA_ROW), BF16), pltpu.VMEM((NJ, D_KV_LAT, NP * PAGE), BF16),
                        pltpu.VMEM((NJ, R, NP * PAGE), F32), pltpu.VMEM((R, LANES), F32)])
    return pl.pallas_call(
        functools.partial(_mla_sample_kernel, NP, NJ, T),
        grid_spec=grid_spec,
        out_shape=jax.ShapeDtypeStruct((B * T, H_MLA * DV_MLA), F32),
        compiler_params=_cparams(("arbitrary", "arbitrary")),
    )(pt_flat, *([cache] * NP), qc_hm, new_rows, wuv_cat)


def _mix_kernel(oret_ref, odsa_ref, omla_ref, x_ref, g1_ref, sh2_ref, sc2_ref, n2_ref,
                wout_ref, wrh_ref, wrl_ref, br_ref, x1_ref, h2_ref, gate_ref):
    y = (_dot(oret_ref[...].astype(BF16), wout_ref[0:384, :])
         + _dot(odsa_ref[...].astype(BF16), wout_ref[384:640, :])
         + _dot(omla_ref[...].astype(BF16), wout_ref[640:1024, :]))
    x1 = x_ref[...] + g1_ref[...] * y
    x1_ref[...] = x1
    rstd = lax.rsqrt(jnp.mean(x1 * x1, axis=-1, keepdims=True) + EPS)
    h2 = (x1 * rstd) * n2_ref[...] * (1.0 + sc2_ref[...]) + sh2_ref[...]
    h2b = h2.astype(BF16)
    h2_ref[...] = h2b
    logits = _dot(h2b, wrh_ref[...]) + _dot(h2b, wrl_ref[...]) + br_ref[...]
    lane = lax.broadcasted_iota(I32, logits.shape, 1)
    is_g = (lane >= N_EXPERTS) & (lane < N_EXPERTS + N_GROUPS)
    gl = jnp.where(is_g, logits, -jnp.inf)
    gmax = jnp.max(gl, axis=-1, keepdims=True)
    grp = jnp.min(jnp.where(gl == gmax, lane - N_EXPERTS, N_GROUPS), axis=-1, keepdims=True)
    g_prob = 1.0 / jnp.sum(jnp.exp(gl - gmax), axis=-1, keepdims=True)
    in_grp = (lane < N_EXPERTS) & ((lane >> 3) == grp)
    el = jnp.where(in_grp, logits, -jnp.inf)
    m1 = jnp.max(el, axis=-1, keepdims=True)
    i1 = jnp.min(jnp.where(el == m1, lane, LANES), axis=-1, keepdims=True)
    el2 = jnp.where(lane == i1, -jnp.inf, el)
    m2 = jnp.max(el2, axis=-1, keepdims=True)
    i2 = jnp.min(jnp.where(el2 == m2, lane, LANES), axis=-1, keepdims=True)
    e2 = jnp.exp(m2 - m1)
    den = 1.0 + e2
    gate_ref[...] = jnp.where(lane == i1, 1.0 / den, jnp.where(lane == i2, e2 / den, 0.0)) * g_prob


def _mix(o_ret, o_dsa, o_mla, x, gate1, shift2, scale2, n2, lw, tm):
    T = x.shape[0]
    per_tok = gate1.shape[0] != 1

    def row(width):
        return pl.BlockSpec((tm, width), lambda i: (i, 0))

    def const(shape):
        return pl.BlockSpec(shape, lambda i: (0, 0))
    mod_spec = row(D_MODEL) if per_tok else const((1, D_MODEL))
    return pl.pallas_call(
        _mix_kernel, grid=(T // tm,),
        in_specs=[row(384), row(256), row(384), row(D_MODEL), mod_spec, mod_spec, mod_spec,
                  const((1, D_MODEL)), const((D_MODEL, D_MODEL)), const((D_MODEL, LANES)),
                  const((D_MODEL, LANES)), const((1, LANES))],
        out_specs=[row(D_MODEL), row(D_MODEL), row(LANES)],
        out_shape=[jax.ShapeDtypeStruct((T, D_MODEL), F32), jax.ShapeDtypeStruct((T, D_MODEL), BF16),
                   jax.ShapeDtypeStruct((T, LANES), F32)],
        compiler_params=_cparams(("arbitrary",)),
    )(o_ret, o_dsa, o_mla, x, gate1, shift2, scale2, n2, lw["w_out"], lw["w_rt_hi"], lw["w_rt_lo"], lw["b_rt"])


def _moe_kernel(h_ref, gate_ref, x_ref, g2_ref, w13_ref, w2_ref, o_ref, acc_ref):
    e = pl.program_id(1)

    @pl.when(e == 0)
    def _():
        acc_ref[...] = jnp.zeros_like(acc_ref)

    gate = gate_ref[...]
    lane = lax.broadcasted_iota(I32, gate.shape, 1)
    g_col = jnp.sum(jnp.where(lane == e, gate, 0.0), axis=-1, keepdims=True)
    au = _dot(h_ref[...], w13_ref[0])
    a, u = au[:, 0:D_EXPERT], au[:, D_EXPERT:]
    act = (a * _sigmoid(a)) * u * g_col
    acc_ref[...] += _dot(act.astype(BF16), w2_ref[0])

    @pl.when(e == pl.num_programs(1) - 1)
    def _():
        o_ref[...] = x_ref[...] + g2_ref[...] * acc_ref[...]


def _moe(h2, gate, x1, gate2, w13, w2, tm):
    T = h2.shape[0]
    per_tok = gate2.shape[0] != 1
    mod_spec = (pl.BlockSpec((tm, D_MODEL), lambda i, e: (i, 0)) if per_tok
                else pl.BlockSpec((1, D_MODEL), lambda i, e: (0, 0)))
    return pl.pallas_call(
        _moe_kernel, grid=(T // tm, N_EXPERTS),
        in_specs=[pl.BlockSpec((tm, D_MODEL), lambda i, e: (i, 0)),
                  pl.BlockSpec((tm, LANES), lambda i, e: (i, 0)),
                  pl.BlockSpec((tm, D_MODEL), lambda i, e: (i, 0)),
                  mod_spec,
                  pl.BlockSpec((1, D_MODEL, 2 * D_EXPERT), lambda i, e: (e, 0, 0)),
                  pl.BlockSpec((1, D_EXPERT, D_MODEL), lambda i, e: (e, 0, 0))],
        out_specs=pl.BlockSpec((tm, D_MODEL), lambda i, e: (i, 0)),
        out_shape=jax.ShapeDtypeStruct((T, D_MODEL), F32),
        scratch_shapes=[pltpu.VMEM((tm, D_MODEL), F32)],
        compiler_params=_cparams(("arbitrary", "arbitrary")),
    )(h2, gate, x1, gate2, w13, w2)


def _rope_tables(pos, d):
    half = d // 2
    inv = ROPE_THETA ** (-jnp.arange(half, dtype=F32) * 2.0 / d)
    ang = pos.astype(F32)[:, None] * inv[None, :]
    cos, sin = jnp.cos(ang), jnp.sin(ang)
    reps = LANES // d
    return (jnp.tile(jnp.concatenate([cos, cos], axis=1), (1, reps)),
            jnp.tile(jnp.concatenate([-sin, sin], axis=1), (1, reps)))


def _pad_cols(w, width):
    return jnp.pad(w, ((0, 0), (0, width - w.shape[1])))


def _layer_weights(l, w_in, ret_gn_g, dsa_qn_g, dsa_kn_g, idx_kn_g, mla_cqn_g, mla_ckvn_g, mla_krn_g,
                   mla_qn_g, w_uq, w_uk, w_uv, w_out, w_group, b_group, w_router, b_router, w1, w3, w2):
    offs = np.cumsum((0,) + IN_SPLITS)
    cols = [w_in[l][:, offs[k]:offs[k + 1]] for k in range(len(IN_SPLITS))]
    rq, rk, rv, rg, dq, dk, dv, iq, ik, iw, cq, ckv, kr = cols
    w_pack = jnp.concatenate([rq, rk, rv, rg, dq, dk, dv, iq, _pad_cols(ik, 128), cq, ckv,
                              _pad_cols(kr, 128), _pad_cols(iw, 128)], axis=1).astype(BF16)
    uq = w_uq[l].reshape(D_Q_LAT, H_MLA, D_QK_MLA)
    uq_pack = jnp.concatenate([uq[:, :, :D_NOPE].reshape(D_Q_LAT, 384), uq[:, :, D_NOPE:].reshape(D_Q_LAT, 192),
                               jnp.zeros((D_Q_LAT, 64), F32)], axis=1).astype(BF16)
    ukbd = jnp.zeros((384, 768), F32)
    for hh in range(H_MLA):
        ukbd = ukbd.at[64 * hh:64 * hh + 64, 128 * hh:128 * hh + 128].set(w_uk[l, hh])
    g_mq = jnp.concatenate([jnp.tile(mla_qn_g[l, :D_NOPE], H_MLA), jnp.tile(mla_qn_g[l, D_NOPE:], H_MLA),
                            jnp.zeros((64,), F32)])[None, :]
    head_of = np.concatenate([np.repeat(np.arange(H_MLA), D_NOPE), np.repeat(np.arange(H_MLA), D_ROPE),
                              np.full((64,), -1)])
    bq = jnp.asarray((head_of[:, None] == head_of[None, :]) & (head_of[:, None] >= 0), BF16)
    h64 = np.arange(256) // 64
    b64 = jnp.asarray(h64[:, None] == h64[None, :], BF16)
    w_rt = jnp.concatenate([w_router[l], w_group[l], jnp.zeros((D_MODEL, LANES - N_EXPERTS - N_GROUPS), F32)], axis=1)
    rt_hi = w_rt.astype(BF16)
    rt_lo = (w_rt - rt_hi.astype(F32)).astype(BF16)
    b_rt = jnp.concatenate([b_router[l], b_group[l], jnp.zeros((LANES - N_EXPERTS - N_GROUPS,), F32)])[None, :]
    return dict(
        w_pack=w_pack, w_uq=uq_pack, w_ukbd=ukbd.astype(BF16), b64=b64, bq=bq,
        g_dq=jnp.tile(dsa_qn_g[l], H_DSA)[None, :],
        g_kv=_pad_cols(dsa_kn_g[l][None, :], 128), g_ik=_pad_cols(idx_kn_g[l][None, :], 128),
        g_cq=mla_cqn_g[l][None, :], g_ckv=mla_ckvn_g[l][None, :], g_kr=_pad_cols(mla_krn_g[l][None, :], 128),
        g_mq=g_mq, gn=ret_gn_g[l].reshape(1, H_RET * DV_RET),
        wuvt=jnp.swapaxes(w_uv[l], 1, 2).astype(BF16),
        wuv_cat=jnp.transpose(w_uv[l], (1, 0, 2)).reshape(D_KV_LAT, H_MLA * DV_MLA).astype(BF16),
        w_out=w_out[l].astype(BF16), w_rt_hi=rt_hi, w_rt_lo=rt_lo, b_rt=b_rt,
        w13=jnp.concatenate([w1[l], w3[l]], axis=-1).astype(BF16), w2=w2[l].astype(BF16))


def _pick_tm(T, pref):
    tm = min(pref, T)
    while T % tm:
        tm //= 2
    return tm


def kernel(x_prompt, x_sample, cache_dsa, cache_mla, state_ret, page_table, c_prompt, c_sample, norm1_g, norm2_g, w_ada, b_ada, w_in, ret_gn_g, dsa_qn_g, dsa_kn_g, idx_kn_g, mla_cqn_g, mla_ckvn_g, mla_krn_g, mla_qn_g, w_uq, w_uk, w_uv, w_out, w_group, b_group, w_router, b_router, w1, w3, w2):
    depth = w_in.shape[0]
    BP, S, _ = x_prompt.shape
    B, T, _ = x_sample.shape
    assert BP == 1 and S % 256 == 0 and T == 8
    n_pages = page_table.shape[1]
    past = n_pages * PAGE
    topk_p = min(TOPK_MAX, S // 4)
    topk_s = min(TOPK_MAX, (past + T) // 4)
    log_gammas = tuple(float(np.log(np.float32(1.0) - np.float32(2.0) ** np.float32(-5.0 - h))) for h in range(H_RET))
    KC = _pick_tm(S, 512)
    assert S % (KC * UNROLL) == 0 and S < ZERO_ATOM and past + PAGE < ZERO_ATOM
    cache_dsa_t = jnp.swapaxes(cache_dsa, 2, 3)
    cache_mla_t = jnp.swapaxes(cache_mla, 2, 3)

    n_rows = 1 + B
    r_pad = (-n_rows) % 8
    c_all = jnp.concatenate([c_prompt, c_sample, jnp.zeros((r_pad, D_MODEL), F32)], axis=0)
    mods = _ada(c_all, w_ada, b_ada)

    tabs_p = _rope_tables(jnp.arange(S), 64) + _rope_tables(jnp.arange(S), 32)
    pos_s = jnp.tile(past + jnp.arange(T), B)
    tabs_s = _rope_tables(pos_s, 64) + _rope_tables(pos_s, 32)
    pt_flat = page_table.reshape(-1)

    xp = x_prompt.reshape(S, D_MODEL)
    xs = x_sample.reshape(B * T, D_MODEL)
    tm_p = KC
    tm_s = _pick_tm(B * T, 256)
    zero_state = jnp.zeros((1, H_RET, DK_RET, DV_RET), F32)
    outs = [[] for _ in range(6)]
    for l in range(depth):
        lw = _layer_weights(l, w_in, ret_gn_g, dsa_qn_g, dsa_kn_g, idx_kn_g, mla_cqn_g, mla_ckvn_g, mla_krn_g,
                            mla_qn_g, w_uq, w_uk, w_uv, w_out, w_group, b_group, w_router, b_router, w1, w3, w2)
        mp = [mods[l, 0:1, k * D_MODEL:(k + 1) * D_MODEL] for k in range(6)]
        ms = [jnp.repeat(mods[l, 1:1 + B, k * D_MODEL:(k + 1) * D_MODEL], T, axis=0) for k in range(6)]
        n1 = norm1_g[l][None, :]
        n2 = norm2_g[l][None, :]

        (rq, rk, rv, rg, dq_hm, iq_hm, _, iwt, drow, kd, ikd, kvt, qc_hm, mrow, kvm, ckvt) = _proj(
            xp, mp[0], mp[1], n1, tabs_p, lw, BF16, tm_p)
        o_ret, st_p = _retention(rq, rk, rv, rg, lw["gn"], zero_state, 128, log_gammas, BF16)
        o_dsa = _dsa_prompt(iq_hm, iwt, dq_hm, ikd, kd, kvt, topk_p, KC)
        o_mla = _mla_prompt(qc_hm, kvm, ckvt, lw["wuvt"], KC)
        x1, h2, gate = _mix(o_ret, o_dsa, o_mla, xp, mp[2], mp[3], mp[4], n2, lw, tm_p)
        xp = _moe(h2, gate, x1, mp[5], lw["w13"], lw["w2"], _pick_tm(S, 1024))
        outs[0].append(drow.reshape(1, S, DSA_ROW))
        outs[2].append(mrow.reshape(1, S, MLA_ROW))
        outs[4].append(st_p)

        (rq, rk, rv, rg, dq_hm, iq_hm, iw, _, drow, _, _, _, qc_hm, mrow, _, _) = _proj(
            xs, ms[0], ms[1], n1, tabs_s, lw, F32, tm_s)
        o_ret, st_s = _retention(rq, rk, rv, rg, lw["gn"], state_ret[l].astype(F32), T, log_gammas, F32)
        o_dsa = _dsa_sample(l, cache_dsa_t, pt_flat, n_pages, iq_hm, iw, dq_hm, drow, T, topk_s)
        o_mla = _mla_sample(l, cache_mla_t, pt_flat, n_pages, qc_hm, mrow, lw["wuv_cat"], T)
        x1, h2, gate = _mix(o_ret, o_dsa, o_mla, xs, ms[2], ms[3], ms[4], n2, lw, tm_s)
        xs = _moe(h2, gate, x1, ms[5], lw["w13"], lw["w2"], _pick_tm(B * T, 1024))
        outs[1].append(drow.reshape(B, T, DSA_ROW))
        outs[3].append(mrow.reshape(B, T, MLA_ROW))
        outs[5].append(st_s)

    return (xp.reshape(1, S, D_MODEL), xs.reshape(B, T, D_MODEL),
            jnp.stack(outs[0]), jnp.stack(outs[1]), jnp.stack(outs[2]), jnp.stack(outs[3]),
            jnp.stack(outs[4]), jnp.stack(outs[5]))
```
